```python
import math
import jax, jax.numpy as jnp
from jax import lax
import numpy as np

D_MODEL = 1024
BATCH = 4
SEQ = 8192
DEPTH = 4

GRID_W = 64
CTX_LEN = 256
N_MIXERS = 3
N_POOL_LAYERS = len(range(0, DEPTH, N_MIXERS))
N_DIFF_LAYERS = len(range(1, DEPTH, N_MIXERS))
N_NAT_LAYERS = len(range(2, DEPTH, N_MIXERS))
D_FF = ((8 * D_MODEL // 3 + 255) // 256) * 256
N_MOD = 9
POOL_WINDOWS = (2, 4, 8, 16)
POOL_GROUPS = len(POOL_WINDOWS)
POOL_GW = D_MODEL // POOL_GROUPS
DIFF_HEADS = 8
DIFF_HD = D_MODEL // DIFF_HEADS // 2
NAT_HEADS = 16
NAT_HD = D_MODEL // NAT_HEADS
NAT_WIN_ROWS = 8
NAT_WIN_COLS = 16
ROPE_THETA = 10000.0
Q_BLOCK = 128
NORM_EPS = 1e-6

kernel_name = 'hybrid_pool_diffattn_natten_macaron_dit'


def rms_norm(x, g):
    xf = x.astype(jnp.float32)
    y = xf * lax.rsqrt(jnp.mean(xf * xf, axis=-1, keepdims=True) + NORM_EPS)
    return (y * g.astype(jnp.float32)).astype(x.dtype)


def modulate(h, shift, scale):
    return h * (1 + scale) + shift


def swiglu(h, w_in, w_out):
    g, u = jnp.split(h @ w_in, 2, axis=-1)
    return (jax.nn.silu(g) * u) @ w_out


def ffn_step(s, g, shift, scale, gate, w_in, w_out):
    return s + 0.5 * gate * swiglu(modulate(rms_norm(s, g), shift, scale), w_in, w_out)


def pool_mix(h, w, scale):
    b, n, d = h.shape
    hf = h.astype(jnp.float32)
    cs = jnp.concatenate([jnp.zeros((b, 1, d), jnp.float32), jnp.cumsum(hf, axis=1)], axis=1)
    t = jnp.arange(n)
    outs = []
    for gi, win in enumerate(POOL_WINDOWS):
        lo = jnp.clip(t - win // 2, 0, n - 1)
        hi = jnp.clip(t - win // 2 + win - 1, 0, n - 1)
        sl = slice(gi * POOL_GW, (gi + 1) * POOL_GW)
        csg = cs[:, :, sl]
        mean = (csg[:, hi + 1] - csg[:, lo]) / (hi - lo + 1).astype(jnp.float32)[None, :, None]
        outs.append(mean - hf[:, :, sl])
    diff = jnp.stack(outs, axis=2).astype(h.dtype)
    y = jnp.einsum('bngc,gce->bnge', diff, w).reshape(b, n, d)
    return y * scale


def axial_angles(n):
    t = jnp.arange(n)
    rows = (t // GRID_W).astype(jnp.float32)
    cols = (t % GRID_W).astype(jnp.float32)
    per_axis = DIFF_HD // 2
    inv = ROPE_THETA ** (-jnp.arange(0, per_axis, 2, dtype=jnp.float32) / per_axis)
    return rows[:, None] * inv, cols[:, None] * inv


def rotate(x, ang):
    m = x.shape[-1] // 2
    x1, x2 = x[..., :m], x[..., m:]
    cos, sin = jnp.cos(ang), jnp.sin(ang)
    return jnp.concatenate([x1 * cos - x2 * sin, x2 * cos + x1 * sin], axis=-1)


def rope_2d(x, row_ang, col_ang):
    shape = (1, x.shape[1]) + (1,) * (x.ndim - 3) + (row_ang.shape[-1],)
    ra, ca = row_ang.reshape(shape), col_ang.reshape(shape)
    xf = x.astype(jnp.float32)
    half = x.shape[-1] // 2
    out = jnp.concatenate([rotate(xf[..., :half], ra), rotate(xf[..., half:], ca)], axis=-1)
    return out.astype(x.dtype)


def diff_attention(h_lat, h_ctx, w_qkv, lam, subln_g, w_o, lam_init, want_ctx):
    b, n, d = h_lat.shape
    scale = DIFF_HD ** -0.5

    def proj(h):
        q, k, v = jnp.split(h @ w_qkv, 3, axis=-1)
        s = h.shape[:2]
        return (q.reshape(s + (DIFF_HEADS, 2, DIFF_HD)), k.reshape(s + (DIFF_HEADS, 2, DIFF_HD)),
                v.reshape(s + (DIFF_HEADS, 2 * DIFF_HD)))

    q_l, k_l, v_l = proj(h_lat)
    q_c, k_c, v_c = proj(h_ctx)
    ra, ca = axial_angles(n)
    q_l = rope_2d(q_l, ra, ca)
    k_l = rope_2d(k_l, ra, ca)
    lf = lam.astype(jnp.float32)
    lam_full = jnp.exp(jnp.sum(lf[0] * lf[1])) - jnp.exp(jnp.sum(lf[2] * lf[3])) + lam_init

    def attend(q, k, v):
        s = jnp.einsum('bqhjd,bkhjd->bhjqk', q, k).astype(jnp.float32) * scale
        p = jax.nn.softmax(s, axis=-1)
        a = p[:, :, 0] - lam_full * p[:, :, 1]
        return jnp.einsum('bhqk,bkhe->bqhe', a.astype(v.dtype), v)

    k_all = jnp.concatenate([k_l, k_c], axis=1)
    v_all = jnp.concatenate([v_l, v_c], axis=1)
    nb = n // Q_BLOCK
    q_blocks = jnp.moveaxis(q_l.reshape(b, nb, Q_BLOCK, DIFF_HEADS, 2, DIFF_HD), 1, 0)
    o_l = lax.map(lambda qb: attend(qb, k_all, v_all), q_blocks)
    o_l = jnp.moveaxis(o_l, 0, 1).reshape(b, n, DIFF_HEADS, 2 * DIFF_HD)

    def finish(o):
        o = rms_norm(o, subln_g) * (1 - lam_init)
        return o.reshape(o.shape[:2] + (d,)) @ w_o

    out_l = finish(o_l)
    out_c = finish(attend(q_c, k_c, v_c)) if want_ctx else None
    return out_l, out_c


def neighbourhood_attention(h_lat, h_ctx, w_qkv, b_qkv, rpb, w_o, b_o, want_ctx):
    b, n, d = h_lat.shape
    n_rows = n // GRID_W
    wr = min(NAT_WIN_ROWS, n_rows)
    wc = min(NAT_WIN_COLS, GRID_W)
    scale = NAT_HD ** -0.5

    def proj(h):
        q, k, v = jnp.split(h @ w_qkv + b_qkv, 3, axis=-1)
        s = h.shape[:2] + (NAT_HEADS, NAT_HD)
        return q.reshape(s) * scale, k.reshape(s), v.reshape(s)

    q_l, k_l, v_l = proj(h_lat)
    q_c, k_c, v_c = proj(h_ctx)
    gshape = (b, n_rows, GRID_W, NAT_HEADS, NAT_HD)
    k_g = k_l.reshape(gshape)
    v_g = v_l.reshape(gshape)
    q_rows = jnp.moveaxis(q_l.reshape(gshape), 1, 0)
    qcol = jnp.arange(GRID_W)
    col_start = jnp.clip(qcol - wc // 2, 0, GRID_W - wc)
    col_idx = col_start[:, None] + jnp.arange(wc)[None, :]
    col_off = col_idx - qcol[:, None] + (NAT_WIN_COLS - 1)

    def row_block(args):
        r, q = args
        rs = jnp.clip(r - wr // 2, 0, n_rows - wr)
        k_band = lax.dynamic_slice_in_dim(k_g, rs, wr, axis=1)
        v_band = lax.dynamic_slice_in_dim(v_g, rs, wr, axis=1)
        k_win = k_band[:, :, col_idx]
        v_win = v_band[:, :, col_idx]
        row_off = rs + jnp.arange(wr) - r + (NAT_WIN_ROWS - 1)
        bias = rpb[:, row_off[:, None, None], col_off[None, :, :]]
        bias = jnp.transpose(bias, (0, 2, 1, 3)).astype(jnp.float32)
        s_win = jnp.einsum('bqhd,bwqchd->bhqwc', q, k_win).astype(jnp.float32) + bias[None]
        s_ctx = jnp.einsum('bqhd,bkhd->bhqk', q, k_c).astype(jnp.float32)
        s = jnp.concatenate([s_win.reshape(b, NAT_HEADS, GRID_W, wr * wc), s_ctx], axis=-1)
        p = jax.nn.softmax(s, axis=-1).astype(v_win.dtype)
        p_win = p[..., :wr * wc].reshape(b, NAT_HEADS, GRID_W, wr, wc)
        p_ctx = p[..., wr * wc:]
        return (jnp.einsum('bhqwc,bwqchd->bqhd', p_win, v_win)
                + jnp.einsum('bhqk,bkhd->bqhd', p_ctx, v_c))

    o = lax.map(row_block, (jnp.arange(n_rows), q_rows))
    out_l = jnp.moveaxis(o, 0, 1).reshape(b, n, d) @ w_o + b_o
    out_c = None
    if want_ctx:
        s = jnp.einsum('bqhd,bkhd->bhqk', q_c, k_c).astype(jnp.float32)
        p = jax.nn.softmax(s, axis=-1).astype(v_c.dtype)
        o_c = jnp.einsum('bhqk,bkhd->bqhd', p, v_c)
        out_c = o_c.reshape(o_c.shape[0], o_c.shape[1], d) @ w_o + b_o
    return out_l, out_c


def setup_inputs(seed: int = 0) -> dict:
    key = jax.random.key(seed)
    ks = jax.random.split(key, 24)
    D = D_MODEL

    def nrm(k, shape, std):
        return jax.random.normal(k, shape, jnp.float32) * std

    return {
        'x': nrm(ks[0], (BATCH, SEQ, D), 1.0),
        'c': nrm(ks[1], (BATCH, D), 1.0),
        'ctx': nrm(ks[2], (BATCH, CTX_LEN, D), 1.0),
        'c_ctx': nrm(ks[3], (D,), 1.0),
        'ada_w': nrm(ks[4], (DEPTH, D, N_MOD * D), 0.5 * D ** -0.5),
        'ada_b': nrm(ks[5], (DEPTH, N_MOD * D), 0.01),
        'norm_g': 1.0 + nrm(ks[6], (DEPTH, 3, D), 0.1),
        'ffn_w_in': nrm(ks[7], (DEPTH, 2, D, 2 * D_FF), D ** -0.5),
        'ffn_w_out': nrm(ks[8], (DEPTH, 2, D_FF, D), D_FF ** -0.5),
        'pool_w': nrm(ks[9], (N_POOL_LAYERS, POOL_GROUPS, POOL_GW, POOL_GW), POOL_GW ** -0.5),
        'pool_scale': 1.0 + nrm(ks[10], (N_POOL_LAYERS, D), 0.1),
        'diff_w_qkv': nrm(ks[11], (N_DIFF_LAYERS, D, 3 * D), D ** -0.5),
        'diff_lam': nrm(ks[12], (N_DIFF_LAYERS, 4, DIFF_HD), 0.1),
        'diff_subln_g': 1.0 + nrm(ks[13], (N_DIFF_LAYERS, 2 * DIFF_HD), 0.1),
        'diff_w_o': nrm(ks[14], (N_DIFF_LAYERS, D, D), D ** -0.5),
        'nat_w_qkv': nrm(ks[15], (N_NAT_LAYERS, D, 3 * D), D ** -0.5),
        'nat_b_qkv': nrm(ks[16], (N_NAT_LAYERS, 3 * D), 0.01),
        'nat_rpb': nrm(ks[17], (N_NAT_LAYERS, NAT_HEADS, 2 * NAT_WIN_ROWS - 1, 2 * NAT_WIN_COLS - 1), 0.1),
        'nat_w_o': nrm(ks[18], (N_NAT_LAYERS, D, D), D ** -0.5),
        'nat_b_o': nrm(ks[19], (N_NAT_LAYERS, D), 0.01),
        'final_g': 1.0 + nrm(ks[20], (D,), 0.1),
    }


def reference(x, c, ctx, c_ctx, ada_w, ada_b, norm_g, ffn_w_in, ffn_w_out, pool_w, pool_scale,
              diff_w_qkv, diff_lam, diff_subln_g, diff_w_o, nat_w_qkv, nat_b_qkv, nat_rpb,
              nat_w_o, nat_b_o, final_g):
    xc = ctx
    s_c = jax.nn.silu(c)
    s_cc = jax.nn.silu(c_ctx)
    for i in range(DEPTH):
        kind = i % N_MIXERS
        j = i // N_MIXERS
        last = i == DEPTH - 1
        update_ctx = not last
        ctx_needed = update_ctx or kind != 0
        mx = jnp.split((s_c @ ada_w[i] + ada_b[i])[:, None, :], N_MOD, axis=-1)
        mc = jnp.split((s_cc @ ada_w[i] + ada_b[i])[None, None, :], N_MOD, axis=-1)
        x = ffn_step(x, norm_g[i, 0], mx[0], mx[1], mx[2], ffn_w_in[i, 0], ffn_w_out[i, 0])
        if ctx_needed:
            xc = ffn_step(xc, norm_g[i, 0], mc[0], mc[1], mc[2], ffn_w_in[i, 0], ffn_w_out[i, 0])
        hx = modulate(rms_norm(x, norm_g[i, 1]), mx[3], mx[4])
        hc = modulate(rms_norm(xc, norm_g[i, 1]), mc[3], mc[4]) if ctx_needed else None
        if kind == 0:
            ox = pool_mix(hx, pool_w[j], pool_scale[j])
            oc = pool_mix(hc, pool_w[j], pool_scale[j]) if update_ctx else None
        elif kind == 1:
            lam_init = 0.8 - 0.6 * math.exp(-0.3 * i)
            ox, oc = diff_attention(hx, hc, diff_w_qkv[j], diff_lam[j], diff_subln_g[j], diff_w_o[j],
                                    lam_init, update_ctx)
        else:
            ox, oc = neighbourhood_attention(hx, hc, nat_w_qkv[j], nat_b_qkv[j], nat_rpb[j],
                                             nat_w_o[j], nat_b_o[j], update_ctx)
        x = x + mx[5] * ox
        if update_ctx:
            xc = xc + mc[5] * oc
        x = ffn_step(x, norm_g[i, 2], mx[6], mx[7], mx[8], ffn_w_in[i, 1], ffn_w_out[i, 1])
        if update_ctx:
            xc = ffn_step(xc, norm_g[i, 2], mc[6], mc[7], mc[8], ffn_w_in[i, 1], ffn_w_out[i, 1])
    return rms_norm(x, final_g)
```

```python
import functools
import math

import numpy as np
import jax
import jax.numpy as jnp
from jax import lax
from jax.experimental import pallas as pl
from jax.experimental.pallas import tpu as pltpu

D_MODEL = 1024
DEPTH = 4
GRID_W = 64
N_MIXERS = 3
D_FF = 2816
N_MOD = 9
POOL_WINDOWS = (2, 4, 8, 16)
POOL_GW = 256
POOL_HALO = 8
DIFF_HEADS = 8
DIFF_HD = 64
NAT_HEADS = 16
NAT_HD = 64
NAT_WIN_ROWS = 8
NAT_WIN_COLS = 16
NAT_QROWS = 4
NAT_BAND = 12
ROPE_THETA = 10000.0
NORM_EPS = 1e-6
NEG = -1e30

LANE = 128
FF_CHUNK = 256
VMEM_LIMIT = 56 * 1024 * 1024

F32 = jnp.float32
BF16 = jnp.bfloat16


def _cparams(n_axes):
    return pltpu.CompilerParams(dimension_semantics=("arbitrary",) * n_axes, vmem_limit_bytes=VMEM_LIMIT)


def _const_spec(shape):
    nd = len(shape)
    return pl.BlockSpec(shape, lambda *_: (0,) * nd, pipeline_mode=pl.Buffered(1))


def _mod_spec(mod):
    if mod.shape[0] == 1:
        return pl.BlockSpec((1, N_MOD, D_MODEL), lambda b, i: (0, 0, 0))
    return pl.BlockSpec((1, N_MOD, D_MODEL), lambda b, i: (b, 0, 0))


def _sigmoid(x):
    return 1.0 / (1.0 + jnp.exp(-x))


def _rms(x):
    return x * lax.rsqrt(jnp.mean(x * x, axis=-1, keepdims=True) + NORM_EPS)


def _norm_mod(x, g, mod_ref, row):
    shift = mod_ref[0, row:row + 1, :]
    scale = mod_ref[0, row + 1:row + 2, :]
    return (_rms(x) * g) * (1.0 + scale) + shift


def _ada_kernel(c_ref, w_ref, b_ref, o_ref):
    c = c_ref[...]
    s = (c * _sigmoid(c)).astype(BF16)
    o_ref[0] = jnp.dot(s, w_ref[0].astype(BF16), preferred_element_type=F32) + b_ref[0]


def _ada_mods(c, c_ctx, ada_w, ada_b):
    d = D_MODEL
    cc = jnp.concatenate([c, c_ctx[None], jnp.zeros((3, d), F32)], axis=0)
    tn = 1024
    return pl.pallas_call(
        _ada_kernel,
        out_shape=jax.ShapeDtypeStruct((DEPTH, 8, N_MOD * d), F32),
        grid=(DEPTH, N_MOD * d // tn),
        in_specs=[pl.BlockSpec((8, d), lambda l, n: (0, 0)),
                  pl.BlockSpec((1, d, tn), lambda l, n: (l, 0, n)),
                  pl.BlockSpec((1, 1, tn), lambda l, n: (l, 0, n))],
        out_specs=pl.BlockSpec((1, 8, tn), lambda l, n: (l, 0, n)),
        compiler_params=_cparams(2), name="ada_mods",
    )(cc, ada_w, ada_b.reshape(DEPTH, 1, N_MOD * d))


def _ffn_kernel(x_ref, mod_ref, g_ref, win_ref, wout_ref, *rest, row, final):
    if final:
        fg_ref, o_ref, a_ref = rest
    else:
        o_ref, a_ref = rest
    x = x_ref[0]
    h = _norm_mod(x, g_ref[...], mod_ref, row).astype(BF16)
    for j in range(D_FF // FF_CHUNK):
        lo = j * FF_CHUNK
        g = jnp.dot(h, win_ref[:, lo:lo + FF_CHUNK], preferred_element_type=F32)
        u = jnp.dot(h, win_ref[:, D_FF + lo:D_FF + lo + FF_CHUNK], preferred_element_type=F32)
        a_ref[:, lo:lo + FF_CHUNK] = ((g * _sigmoid(g)) * u).astype(BF16)
    y = jnp.dot(a_ref[...], wout_ref[...], preferred_element_type=F32)
    out = x + (0.5 * mod_ref[0, row + 2:row + 3, :]) * y
    if final:
        out = _rms(out) * fg_ref[...]
    o_ref[0] = out


def _ffn(x, mod, g, w_in, w_out, row, final_g=None):
    bx, sx, d = x.shape
    tm = min(512, sx)
    final = final_g is not None
    in_specs = [pl.BlockSpec((1, tm, d), lambda b, i: (b, i, 0)), _mod_spec(mod),
                _const_spec((1, d)), _const_spec((d, 2 * D_FF)), _const_spec((D_FF, d))]
    args = [x, mod, g.reshape(1, d), w_in, w_out]
    if final:
        in_specs.append(_const_spec((1, d)))
        args.append(final_g.reshape(1, d))
    return pl.pallas_call(
        functools.partial(_ffn_kernel, row=row, final=final),
        out_shape=jax.ShapeDtypeStruct(x.shape, F32),
        grid=(bx, sx // tm),
        in_specs=in_specs,
        out_specs=pl.BlockSpec((1, tm, d), lambda b, i: (b, i, 0)),
        scratch_shapes=[pltpu.VMEM((tm, D_FF), BF16)],
        compiler_params=_cparams(2), name="ffn",
    )(*args)


def _pool_kernel(x_ref, xp_ref, xn_ref, mod_ref, g_ref, pw_ref, ps_ref, o_ref, ext_ref, *, seq):
    i = pl.program_id(1)
    last = pl.num_programs(1) - 1
    tm = x_ref.shape[1]
    g = g_ref[...]
    x = x_ref[0]
    h = _norm_mod(x, g, mod_ref, 3)
    hp = _norm_mod(xp_ref[0], g, mod_ref, 3)
    hn = _norm_mod(xn_ref[0], g, mod_ref, 3)
    ext_ref[0:POOL_HALO, :] = jnp.where(i > 0, hp, 0.0)
    ext_ref[POOL_HALO:POOL_HALO + tm, :] = h
    ext_ref[POOL_HALO + tm:, :] = jnp.where(i < last, hn, 0.0)
    t = i * tm + lax.broadcasted_iota(jnp.int32, (tm, 1), 0)
    gate = mod_ref[0, 5:6, :]
    for gi, win in enumerate(POOL_WINDOWS):
        c0 = gi * POOL_GW
        lo_off = -(win // 2)
        acc = ext_ref[POOL_HALO + lo_off:POOL_HALO + lo_off + tm, c0:c0 + POOL_GW]
        for o in range(lo_off + 1, lo_off + win):
            acc = acc + ext_ref[POOL_HALO + o:POOL_HALO + o + tm, c0:c0 + POOL_GW]
        lo = jnp.clip(t + lo_off, 0, seq - 1)
        hi = jnp.clip(t + lo_off + win - 1, 0, seq - 1)
        cnt = (hi - lo + 1).astype(F32)
        diff = (acc / cnt - h[:, c0:c0 + POOL_GW]).astype(BF16)
        y = jnp.dot(diff, pw_ref[gi], preferred_element_type=F32) * ps_ref[:, c0:c0 + POOL_GW]
        o_ref[0, :, c0:c0 + POOL_GW] = x[:, c0:c0 + POOL_GW] + gate[:, c0:c0 + POOL_GW] * y


def _pool(x, mod, g, pool_w, pool_scale):
    bx, sx, d = x.shape
    tm = min(512, sx)
    nb = tm // POOL_HALO
    n8 = sx // POOL_HALO
    return pl.pallas_call(
        functools.partial(_pool_kernel, seq=sx),
        out_shape=jax.ShapeDtypeStruct(x.shape, F32),
        grid=(bx, sx // tm),
        in_specs=[pl.BlockSpec((1, tm, d), lambda b, i: (b, i, 0)),
                  pl.BlockSpec((1, POOL_HALO, d), lambda b, i: (b, jnp.maximum(i * nb - 1, 0), 0)),
                  pl.BlockSpec((1, POOL_HALO, d), lambda b, i: (b, jnp.minimum((i + 1) * nb, n8 - 1), 0)),
                  _mod_spec(mod), _const_spec((1, d)),
                  _const_spec((len(POOL_WINDOWS), POOL_GW, POOL_GW)), _const_spec((1, d))],
        out_specs=pl.BlockSpec((1, tm, d), lambda b, i: (b, i, 0)),
        scratch_shapes=[pltpu.VMEM((tm + 2 * POOL_HALO, d), F32)],
        compiler_params=_cparams(2), name="pool_mix",
    )(x, x, x, mod, g.reshape(1, d), pool_w, pool_scale.reshape(1, d))


def _rope_table_kernel(inv_ref, cos_ref, sin_ref):
    tm = cos_ref.shape[0]
    t = pl.program_id(0) * tm + lax.broadcasted_iota(jnp.int32, (tm, LANE), 0)
    lane = lax.broadcasted_iota(jnp.int32, (tm, LANE), 1)
    pos = jnp.where((lane & 63) < 32, lax.shift_right_logical(t, 6), t & (GRID_W - 1)).astype(F32)
    ang = pos * inv_ref[...]
    cos_ref[...] = jnp.cos(ang)
    sin_ref[...] = jnp.where((lane & 31) < 16, -jnp.sin(ang), jnp.sin(ang))


def _rope_tables(seq):
    per_axis = DIFF_HD // 2
    inv = ROPE_THETA ** (-jnp.arange(0, per_axis, 2, dtype=F32) / per_axis)
    inv_lane = jnp.tile(inv, LANE // inv.shape[0]).reshape(1, LANE)
    tm = 1024
    return pl.pallas_call(
        _rope_table_kernel,
        out_shape=(jax.ShapeDtypeStruct((seq, LANE), F32),) * 2,
        grid=(seq // tm,),
        in_specs=[pl.BlockSpec((1, LANE), lambda i: (0, 0))],
        out_specs=(pl.BlockSpec((tm, LANE), lambda i: (i, 0)),) * 2,
        compiler_params=_cparams(1), name="rope_tables",
    )(inv_lane)


def _qkv_kernel(x_ref, mod_ref, g_ref, wqk_ref, wvt_ref, *rest, rope, bias, scale):
    rest = list(rest)
    if bias:
        bqk_ref, bvt_ref = rest[:2]
        rest = rest[2:]
    if rope:
        cos_ref, sin_ref = rest[:2]
        rest = rest[2:]
    q_ref, k_ref, vt_ref = rest
    d = x_ref.shape[2]
    h = _norm_mod(x_ref[0], g_ref[...], mod_ref, 3).astype(BF16)
    vt = lax.dot_general(wvt_ref[...], h, (((1,), (1,)), ((), ())), preferred_element_type=F32)
    if bias:
        vt = vt + bvt_ref[...]
    vt_ref[0] = vt.astype(BF16)
    if rope:
        cos = cos_ref[...]
        sin = sin_ref[...]
        lane = lax.broadcasted_iota(jnp.int32, cos.shape, 1)
        first_half = (lane & 31) < 16
    for cb in range(2 * d // FF_CHUNK):
        c0 = cb * FF_CHUNK
        t2 = jnp.dot(h, wqk_ref[:, c0:c0 + FF_CHUNK], preferred_element_type=F32)
        if bias:
            t2 = t2 + bqk_ref[:, c0:c0 + FF_CHUNK]
        for half in range(FF_CHUNK // LANE):
            t = t2[:, half * LANE:(half + 1) * LANE]
            if rope:
                partner = jnp.where(first_half, pltpu.roll(t, LANE - 16, 1), pltpu.roll(t, 16, 1))
                t = t * cos + partner * sin
            col = c0 + half * LANE
            if col < d:
                q_ref[0, :, col:col + LANE] = (t * scale).astype(BF16)
            else:
                k_ref[0, :, col - d:col - d + LANE] = t.astype(BF16)


def _qkv(x, mod, g, w_qkv, scale, b_qkv=None, rope_tabs=None):
    bx, sx, d = x.shape
    tm = min(512, sx)
    wqk = w_qkv[:, :2 * d].astype(BF16)
    wvt = w_qkv[:, 2 * d:].T.astype(BF16)
    in_specs = [pl.BlockSpec((1, tm, d), lambda b, i: (b, i, 0)), _mod_spec(mod), _const_spec((1, d)),
                _const_spec((d, 2 * d)), _const_spec((d, d))]
    args = [x, mod, g.reshape(1, d), wqk, wvt]
    if b_qkv is not None:
        in_specs += [_const_spec((1, 2 * d)), _const_spec((d, 1))]
        args += [b_qkv[:2 * d].reshape(1, 2 * d), b_qkv[2 * d:].reshape(d, 1)]
    if rope_tabs is not None:
        in_specs += [pl.BlockSpec((tm, LANE), lambda b, i: (i, 0))] * 2
        args += list(rope_tabs)
    tok = pl.BlockSpec((1, tm, d), lambda b, i: (b, i, 0))
    return pl.pallas_call(
        functools.partial(_qkv_kernel, rope=rope_tabs is not None, bias=b_qkv is not None, scale=scale),
        out_shape=(jax.ShapeDtypeStruct((bx, sx, d), BF16), jax.ShapeDtypeStruct((bx, sx, d), BF16),
                   jax.ShapeDtypeStruct((bx, d, sx), BF16)),
        grid=(bx, sx // tm),
        in_specs=in_specs,
        out_specs=(tok, tok, pl.BlockSpec((1, d, tm), lambda b, i: (b, 0, i))),
        compiler_params=_cparams(2), name="qkv_proj",
    )(*args)


def _diff_attn_kernel(lam_ref, g_ref, q_ref, kc_ref, vct_ref, *rest, lam_init, tk):
    if len(rest) == 4:
        k_ref, vt_ref, o_ref, acc_ref = rest
    else:
        k_ref = vt_ref = None
        o_ref, acc_ref = rest
    tq = q_ref.shape[1]
    q = q_ref[0]
    lane = lax.broadcasted_iota(jnp.int32, q.shape, 1)
    zero = jnp.zeros_like(q)
    qq = jnp.concatenate([jnp.where(lane < DIFF_HD, q, zero), jnp.where(lane >= DIFF_HD, q, zero)], axis=0)

    def scores(k_t):
        return lax.dot_general(k_t, qq, (((1,), (1,)), ((), ())), preferred_element_type=F32)

    s = scores(kc_ref[0])
    m = jnp.max(s, axis=0, keepdims=True)
    p = jnp.exp(s - m)
    l = jnp.sum(p, axis=0, keepdims=True)
    acc_ref[...] = jnp.dot(vct_ref[0], p.astype(BF16), preferred_element_type=F32)

    if k_ref is not None:
        def body(c, carry):
            m, l = carry
            off = pl.multiple_of(c * tk, tk)
            s = scores(k_ref[0, pl.ds(off, tk), :])
            m_new = jnp.maximum(m, jnp.max(s, axis=0, keepdims=True))
            alpha = jnp.exp(m - m_new)
            p = jnp.exp(s - m_new)
            l = alpha * l + jnp.sum(p, axis=0, keepdims=True)
            pv = jnp.dot(vt_ref[0, :, pl.ds(off, tk)], p.astype(BF16), preferred_element_type=F32)
            acc_ref[...] = alpha * acc_ref[...] + pv
            return m_new, l

        m, l = lax.fori_loop(0, k_ref.shape[1] // tk, body, (m, l))

    lf = lam_ref[...]
    lam = (jnp.exp(jnp.sum(lf[0:1] * lf[1:2], axis=-1, keepdims=True))
           - jnp.exp(jnp.sum(lf[2:3] * lf[3:4], axis=-1, keepdims=True)) + lam_init)
    acc = acc_ref[...]
    inv = 1.0 / l
    o = acc[:, :tq] * inv[:, :tq] - lam * (acc[:, tq:] * inv[:, tq:])
    o = o * lax.rsqrt(jnp.mean(o * o, axis=0, keepdims=True) + NORM_EPS)
    o = (o * g_ref[...]) * (1.0 - lam_init)
    o_ref[0] = o.T.astype(BF16)


def _diff_attn(lam, subln_g, q, kc, vct, k, vt, lam_init):
    bq, sq, d = q.shape
    n_ctx = kc.shape[1]
    tq = 256
    hw = 2 * DIFF_HD
    in_specs = [_const_spec((4, DIFF_HD)), _const_spec((hw, 1)),
                pl.BlockSpec((1, tq, hw), lambda b, h, i: (b, i, h)),
                pl.BlockSpec((1, n_ctx, hw), lambda b, h, i: (b, 0, h)),
                pl.BlockSpec((1, hw, n_ctx), lambda b, h, i: (b, h, 0))]
    args = [lam, subln_g.reshape(hw, 1), q, kc, vct]
    if k is not None:
        s = k.shape[1]
        in_specs += [pl.BlockSpec((1, s, hw), lambda b, h, i: (b, 0, h)),
                     pl.BlockSpec((1, hw, s), lambda b, h, i: (b, h, 0))]
        args += [k, vt]
    return pl.pallas_call(
        functools.partial(_diff_attn_kernel, lam_init=lam_init, tk=512),
        out_shape=jax.ShapeDtypeStruct((bq, sq, d), BF16),
        grid=(bq, DIFF_HEADS, sq // tq),
        in_specs=in_specs,
        out_specs=pl.BlockSpec((1, tq, hw), lambda b, h, i: (b, i, h)),
        scratch_shapes=[pltpu.VMEM((hw, 2 * tq), F32)],
        compiler_params=_cparams(3), name="diff_attn",
    )(*args)


def _nat_bias_rows():
    table = []
    for v, shift in enumerate((0, 4, 8)):
        rows = []
        for i in range(NAT_BAND):
            per_d = []
            for dq in range(NAT_QROWS):
                first = (0, dq, 4)[v]
                ok = first <= i < first + NAT_WIN_ROWS
                per_d.append(i - shift - dq + NAT_WIN_ROWS - 1 if ok else None)
            rows.append(per_d)
        table.append(rows)
    return table


def _nat_toeplitz_kernel(r_ref, o_ref):
    m, n = o_ref.shape

    def window(shape):
        col = lax.broadcasted_iota(jnp.int32, shape, 1)
        kc = lax.shift_right_logical(col, 7)
        qc = col & (GRID_W - 1)
        cs = jnp.clip(qc - NAT_WIN_COLS // 2, 0, GRID_W - NAT_WIN_COLS)
        return jnp.where(kc >= cs, jnp.where(kc < cs + NAT_WIN_COLS, kc - qc + NAT_WIN_COLS - 1, -1), -1)

    j = lax.broadcasted_iota(jnp.int32, (LANE, n), 0)
    onehot = jnp.where(j == window((LANE, n)), 1.0, 0.0).astype(BF16)
    r = r_ref[...]
    r_hi = r.astype(BF16)
    r1 = r - r_hi.astype(F32)
    r_mid = r1.astype(BF16)
    r_lo = (r1 - r_mid.astype(F32)).astype(BF16)
    t = (jnp.dot(r_hi, onehot, preferred_element_type=F32) + jnp.dot(r_mid, onehot, preferred_element_type=F32)
         + jnp.dot(r_lo, onehot, preferred_element_type=F32))
    o_ref[...] = jnp.where(window((m, n)) >= 0, t, NEG)


def _nat_bias_kernel(t_ref, o_ref):
    lane = lax.broadcasted_iota(jnp.int32, (GRID_W, LANE), 1)
    neg = jnp.full((GRID_W, LANE), NEG, F32)
    rows = _nat_bias_rows()
    for v in range(3):
        for i in range(NAT_BAND):
            for pair in range(NAT_QROWS // 2):
                a_l, a_r = rows[v][i][2 * pair], rows[v][i][2 * pair + 1]
                left = neg if a_l is None else t_ref[0, a_l]
                right = neg if a_r is None else t_ref[0, a_r]
                o_ref[0, v, i * GRID_W:(i + 1) * GRID_W, pair * LANE:(pair + 1) * LANE] = (
                    jnp.where(lane < GRID_W, left, right))


def _nat_bias(rpb):
    nh, nr, nc = rpb.shape
    r2 = jnp.pad(rpb, ((0, 0), (0, 16 - nr), (0, LANE - nc))).reshape(nh * 16, LANE)
    n = GRID_W * LANE
    t = pl.pallas_call(
        _nat_toeplitz_kernel,
        out_shape=jax.ShapeDtypeStruct((nh * 16, n), F32),
        grid=(1,),
        in_specs=[pl.BlockSpec((nh * 16, LANE), lambda i: (0, 0))],
        out_specs=pl.BlockSpec((nh * 16, n), lambda i: (0, 0)),
        compiler_params=_cparams(1), name="nat_toeplitz",
    )(r2)
    t = t.reshape(nh, 16, GRID_W, LANE)
    nk, nq = NAT_BAND * GRID_W, NAT_QROWS * GRID_W
    return pl.pallas_call(
        _nat_bias_kernel,
        out_shape=jax.ShapeDtypeStruct((nh, 3, nk, nq), F32),
        grid=(nh,),
        in_specs=[pl.BlockSpec((1, 16, GRID_W, LANE), lambda h: (h, 0, 0, 0))],
        out_specs=pl.BlockSpec((1, 3, nk, nq), lambda h: (h, 0, 0, 0)),
        compiler_params=_cparams(1), name="nat_bias",
    )(t)


def _nat_attn_kernel(q_ref, kc_ref, vct_ref, *rest, n_rows):
    if len(rest) == 4:
        bias_ref, k_ref, vt_ref, o_ref = rest
    else:
        bias_ref = k_ref = vt_ref = None
        (o_ref,) = rest
    q = q_ref[0]
    lane = lax.broadcasted_iota(jnp.int32, q.shape, 1)
    zero = jnp.zeros_like(q)
    dn = (((1,), (1,)), ((), ()))
    if k_ref is not None:
        blk = pl.program_id(2)
        start = jnp.clip(blk * NAT_QROWS - NAT_WIN_ROWS // 2, 0, n_rows - NAT_BAND)
        nk = NAT_BAND * GRID_W
        off = pl.multiple_of(start * GRID_W, NAT_QROWS * GRID_W)
        kb = k_ref[0, pl.ds(off, nk), :]
    outs = []
    for hh in range(LANE // NAT_HD):
        qh = jnp.where((lane >= hh * NAT_HD) & (lane < (hh + 1) * NAT_HD), q, zero)
        rows = slice(hh * NAT_HD, (hh + 1) * NAT_HD)
        s_c = lax.dot_general(kc_ref[0], qh, dn, preferred_element_type=F32)
        m = jnp.max(s_c, axis=0, keepdims=True)
        if k_ref is not None:
            s_b = lax.dot_general(kb, qh, dn, preferred_element_type=F32) + bias_ref[hh, 0]
            m = jnp.maximum(m, jnp.max(s_b, axis=0, keepdims=True))
        p_c = jnp.exp(s_c - m)
        l = jnp.sum(p_c, axis=0, keepdims=True)
        o = jnp.dot(vct_ref[0, rows, :], p_c.astype(BF16), preferred_element_type=F32)
        if k_ref is not None:
            p_b = jnp.exp(s_b - m)
            l = l + jnp.sum(p_b, axis=0, keepdims=True)
            o = o + jnp.dot(vt_ref[0, rows, pl.ds(off, nk)], p_b.astype(BF16), preferred_element_type=F32)
        outs.append(o / l)
    o_ref[0] = jnp.concatenate(outs, axis=0).T.astype(BF16)


def _nat_attn(q, kc, vct, bias=None, k=None, vt=None):
    bq, sq, d = q.shape
    n_ctx = kc.shape[1]
    tq = NAT_QROWS * GRID_W
    n_blk = sq // tq
    in_specs = [pl.BlockSpec((1, tq, LANE), lambda hp, b, i: (b, i, hp)),
                pl.BlockSpec((1, n_ctx, LANE), lambda hp, b, i: (b, 0, hp)),
                pl.BlockSpec((1, LANE, n_ctx), lambda hp, b, i: (b, hp, 0))]
    args = [q, kc, vct]
    n_rows = None
    if k is not None:
        s = k.shape[1]
        n_rows = s // GRID_W
        nk = NAT_BAND * GRID_W
        variant = lambda i: jnp.where(i == 0, 0, jnp.where(i == n_blk - 1, 2, 1))
        in_specs += [pl.BlockSpec((LANE // NAT_HD, 1, nk, tq), lambda hp, b, i: (hp, variant(i), 0, 0)),
                     pl.BlockSpec((1, s, LANE), lambda hp, b, i: (b, 0, hp)),
                     pl.BlockSpec((1, LANE, s), lambda hp, b, i: (b, hp, 0))]
        args += [bias, k, vt]
    return pl.pallas_call(
        functools.partial(_nat_attn_kernel, n_rows=n_rows),
        out_shape=jax.ShapeDtypeStruct((bq, sq, d), BF16),
        grid=(d // LANE, bq, n_blk),
        in_specs=in_specs,
        out_specs=pl.BlockSpec((1, tq, LANE), lambda hp, b, i: (b, i, hp)),
        compiler_params=_cparams(3), name="nat_attn",
    )(*args)


def _out_proj_kernel(x_ref, o_ref_in, mod_ref, w_ref, *rest, bias):
    if bias:
        b_ref, out_ref = rest
    else:
        (out_ref,) = rest
    y = jnp.dot(o_ref_in[0], w_ref[...], preferred_element_type=F32)
    if bias:
        y = y + b_ref[...]
    out_ref[0] = x_ref[0] + mod_ref[0, 5:6, :] * y


def _out_proj(x, o, mod, w_o, b_o=None):
    bx, sx, d = x.shape
    tm = min(512, sx)
    tok = pl.BlockSpec((1, tm, d), lambda b, i: (b, i, 0))
    in_specs = [tok, tok, _mod_spec(mod), _const_spec((d, d))]
    args = [x, o, mod, w_o.astype(BF16)]
    if b_o is not None:
        in_specs.append(_const_spec((1, d)))
        args.append(b_o.reshape(1, d))
    return pl.pallas_call(
        functools.partial(_out_proj_kernel, bias=b_o is not None),
        out_shape=jax.ShapeDtypeStruct(x.shape, F32),
        grid=(bx, sx // tm),
        in_specs=in_specs, out_specs=tok,
        compiler_params=_cparams(2), name="out_proj",
    )(*args)


def kernel(x, c, ctx, c_ctx, ada_w, ada_b, norm_g, ffn_w_in, ffn_w_out, pool_w, pool_scale,
           diff_w_qkv, diff_lam, diff_subln_g, diff_w_o, nat_w_qkv, nat_b_qkv, nat_rpb,
           nat_w_o, nat_b_o, final_g):
    b, seq, d = x.shape
    mods = _ada_mods(c, c_ctx, ada_w, ada_b)
    xc = ctx
    for i in range(DEPTH):
        kind = i % N_MIXERS
        j = i // N_MIXERS
        last = i == DEPTH - 1
        update_ctx = not last
        ctx_needed = update_ctx or kind != 0
        mx = mods[i, :b].reshape(b, N_MOD, d)
        mc = mods[i, b:b + 1].reshape(1, N_MOD, d)
        w_in = ffn_w_in[i].astype(BF16)
        w_out = ffn_w_out[i].astype(BF16)

        x = _ffn(x, mx, norm_g[i, 0], w_in[0], w_out[0], 0)
        if ctx_needed:
            xc = _ffn(xc, mc, norm_g[i, 0], w_in[0], w_out[0], 0)

        if kind == 0:
            pw = pool_w[j].astype(BF16)
            x = _pool(x, mx, norm_g[i, 1], pw, pool_scale[j])
            if update_ctx:
                xc = _pool(xc, mc, norm_g[i, 1], pw, pool_scale[j])
        elif kind == 1:
            lam_init = 0.8 - 0.6 * math.exp(-0.3 * i)
            scale = DIFF_HD ** -0.5
            q_l, k_l, vt_l = _qkv(x, mx, norm_g[i, 1], diff_w_qkv[j], scale, rope_tabs=_rope_tables(seq))
            q_c, k_c, vt_c = _qkv(xc, mc, norm_g[i, 1], diff_w_qkv[j], scale)
            o_l = _diff_attn(diff_lam[j], diff_subln_g[j], q_l, k_c, vt_c, k_l, vt_l, lam_init)
            x = _out_proj(x, o_l, mx, diff_w_o[j])
            if update_ctx:
                o_c = _diff_attn(diff_lam[j], diff_subln_g[j], q_c, k_c, vt_c, None, None, lam_init)
                xc = _out_proj(xc, o_c, mc, diff_w_o[j])
        else:
            scale = NAT_HD ** -0.5
            q_l, k_l, vt_l = _qkv(x, mx, norm_g[i, 1], nat_w_qkv[j], scale, b_qkv=nat_b_qkv[j])
            q_c, k_c, vt_c = _qkv(xc, mc, norm_g[i, 1], nat_w_qkv[j], scale, b_qkv=nat_b_qkv[j])
            o_l = _nat_attn(q_l, k_c, vt_c, _nat_bias(nat_rpb[j]), k_l, vt_l)
            x = _out_proj(x, o_l, mx, nat_w_o[j], nat_b_o[j])
            if update_ctx:
                o_c = _nat_attn(q_c, k_c, vt_c)
                xc = _out_proj(xc, o_c, mc, nat_w_o[j], nat_b_o[j])

        x = _ffn(x, mx, norm_g[i, 2], w_in[1], w_out[1], 6, final_g=final_g if last else None)
        if update_ctx:
            xc = _ffn(xc, mc, norm_g[i, 2], w_in[1], w_out[1], 6)
    return x
```

```python
import functools
import math

import numpy as np
import jax
import jax.numpy as jnp
from jax import lax
from jax.experimental import pallas as pl
from jax.experimental.pallas import tpu as pltpu

D_MODEL = 1024
DEPTH = 4
GRID_W = 64
N_MIXERS = 3
D_FF = 2816
N_MOD = 9
POOL_WINDOWS = (2, 4, 8, 16)
POOL_GW = 256
POOL_HALO = 8
DIFF_HEADS = 8
DIFF_HD = 64
NAT_HEADS = 16
NAT_HD = 64
NAT_WIN_ROWS = 8
NAT_WIN_COLS = 16
NAT_QROWS = 4
NAT_BAND = 12
ROPE_THETA = 10000.0
NORM_EPS = 1e-6
NEG = -1e30
LOG2_E = math.log2(math.e)

LANE = 128
FF_CHUNK = 256
VMEM_LIMIT = 56 * 1024 * 1024

F32 = jnp.float32
BF16 = jnp.bfloat16


def _cparams(n_axes):
    return pltpu.CompilerParams(dimension_semantics=("arbitrary",) * n_axes, vmem_limit_bytes=VMEM_LIMIT)


def _const_spec(shape):
    nd = len(shape)
    return pl.BlockSpec(shape, lambda *_: (0,) * nd, pipeline_mode=pl.Buffered(1))


def _mod_spec(mod):
    if mod.shape[0] == 1:
        return pl.BlockSpec((1, N_MOD, D_MODEL), lambda b, i: (0, 0, 0))
    return pl.BlockSpec((1, N_MOD, D_MODEL), lambda b, i: (b, 0, 0))


def _sigmoid(x):
    return 1.0 / (1.0 + jnp.exp(-x))


def _rms(x):
    return x * lax.rsqrt(jnp.mean(x * x, axis=-1, keepdims=True) + NORM_EPS)


def _norm_mod(x, g, mod_ref, row):
    shift = mod_ref[0, row:row + 1, :]
    scale = mod_ref[0, row + 1:row + 2, :]
    return (_rms(x) * g) * (1.0 + scale) + shift


def _ada_kernel(c_ref, w_ref, b_ref, o_ref):
    c = c_ref[...]
    s = (c * _sigmoid(c)).astype(BF16)
    o_ref[0] = jnp.dot(s, w_ref[0].astype(BF16), preferred_element_type=F32) + b_ref[0]


def _ada_mods(c, c_ctx, ada_w, ada_b):
    d = D_MODEL
    cc = jnp.concatenate([c, c_ctx[None], jnp.zeros((3, d), F32)], axis=0)
    tn = 1024
    return pl.pallas_call(
        _ada_kernel,
        out_shape=jax.ShapeDtypeStruct((DEPTH, 8, N_MOD * d), F32),
        grid=(DEPTH, N_MOD * d // tn),
        in_specs=[pl.BlockSpec((8, d), lambda l, n: (0, 0)),
                  pl.BlockSpec((1, d, tn), lambda l, n: (l, 0, n)),
                  pl.BlockSpec((1, 1, tn), lambda l, n: (l, 0, n))],
        out_specs=pl.BlockSpec((1, 8, tn), lambda l, n: (l, 0, n)),
        compiler_params=_cparams(2), name="ada_mods",
    )(cc, ada_w, ada_b.reshape(DEPTH, 1, N_MOD * d))


def _ffn_kernel(x_ref, mod_ref, g_ref, win_ref, wout_ref, *rest, row, final):
    if final:
        fg_ref, o_ref, a_ref = rest
    else:
        o_ref, a_ref = rest
    x = x_ref[0]
    h = _norm_mod(x, g_ref[...], mod_ref, row).astype(BF16)
    for j in range(D_FF // FF_CHUNK):
        lo = j * FF_CHUNK
        g = jnp.dot(h, win_ref[:, lo:lo + FF_CHUNK], preferred_element_type=F32)
        u = jnp.dot(h, win_ref[:, D_FF + lo:D_FF + lo + FF_CHUNK], preferred_element_type=F32)
        a_ref[:, lo:lo + FF_CHUNK] = ((g * _sigmoid(g)) * u).astype(BF16)
    y = jnp.dot(a_ref[...], wout_ref[...], preferred_element_type=F32)
    out = x + (0.5 * mod_ref[0, row + 2:row + 3, :]) * y
    if final:
        out = _rms(out) * fg_ref[...]
    o_ref[0] = out


def _ffn(x, mod, g, w_in, w_out, row, final_g=None):
    bx, sx, d = x.shape
    tm = min(512, sx)
    final = final_g is not None
    in_specs = [pl.BlockSpec((1, tm, d), lambda b, i: (b, i, 0)), _mod_spec(mod),
                _const_spec((1, d)), _const_spec((d, 2 * D_FF)), _const_spec((D_FF, d))]
    args = [x, mod, g.reshape(1, d), w_in, w_out]
    if final:
        in_specs.append(_const_spec((1, d)))
        args.append(final_g.reshape(1, d))
    return pl.pallas_call(
        functools.partial(_ffn_kernel, row=row, final=final),
        out_shape=jax.ShapeDtypeStruct(x.shape, F32),
        grid=(bx, sx // tm),
        in_specs=in_specs,
        out_specs=pl.BlockSpec((1, tm, d), lambda b, i: (b, i, 0)),
        scratch_shapes=[pltpu.VMEM((tm, D_FF), BF16)],
        compiler_params=_cparams(2), name="ffn",
    )(*args)


def _pool_kernel(x_ref, xp_ref, xn_ref, mod_ref, g_ref, pw_ref, ps_ref, o_ref, ext_ref, *, seq):
    i = pl.program_id(1)
    last = pl.num_programs(1) - 1
    tm = x_ref.shape[1]
    g = g_ref[...]
    x = x_ref[0]
    h = _norm_mod(x, g, mod_ref, 3)
    hp = _norm_mod(xp_ref[0], g, mod_ref, 3)
    hn = _norm_mod(xn_ref[0], g, mod_ref, 3)
    ext_ref[0:POOL_HALO, :] = jnp.where(i > 0, hp, 0.0)
    ext_ref[POOL_HALO:POOL_HALO + tm, :] = h
    ext_ref[POOL_HALO + tm:, :] = jnp.where(i < last, hn, 0.0)
    t = i * tm + lax.broadcasted_iota(jnp.int32, (tm, 1), 0)
    gate = mod_ref[0, 5:6, :]
    for gi, win in enumerate(POOL_WINDOWS):
        c0 = gi * POOL_GW
        lo_off = -(win // 2)
        acc = ext_ref[POOL_HALO + lo_off:POOL_HALO + lo_off + tm, c0:c0 + POOL_GW]
        for o in range(lo_off + 1, lo_off + win):
            acc = acc + ext_ref[POOL_HALO + o:POOL_HALO + o + tm, c0:c0 + POOL_GW]
        lo = jnp.clip(t + lo_off, 0, seq - 1)
        hi = jnp.clip(t + lo_off + win - 1, 0, seq - 1)
        cnt = (hi - lo + 1).astype(F32)
        diff = (acc / cnt - h[:, c0:c0 + POOL_GW]).astype(BF16)
        y = jnp.dot(diff, pw_ref[gi], preferred_element_type=F32) * ps_ref[:, c0:c0 + POOL_GW]
        o_ref[0, :, c0:c0 + POOL_GW] = x[:, c0:c0 + POOL_GW] + gate[:, c0:c0 + POOL_GW] * y


def _pool(x, mod, g, pool_w, pool_scale):
    bx, sx, d = x.shape
    tm = min(512, sx)
    nb = tm // POOL_HALO
    n8 = sx // POOL_HALO
    return pl.pallas_call(
        functools.partial(_pool_kernel, seq=sx),
        out_shape=jax.ShapeDtypeStruct(x.shape, F32),
        grid=(bx, sx // tm),
        in_specs=[pl.BlockSpec((1, tm, d), lambda b, i: (b, i, 0)),
                  pl.BlockSpec((1, POOL_HALO, d), lambda b, i: (b, jnp.maximum(i * nb - 1, 0), 0)),
                  pl.BlockSpec((1, POOL_HALO, d), lambda b, i: (b, jnp.minimum((i + 1) * nb, n8 - 1), 0)),
                  _mod_spec(mod), _const_spec((1, d)),
                  _const_spec((len(POOL_WINDOWS), POOL_GW, POOL_GW)), _const_spec((1, d))],
        out_specs=pl.BlockSpec((1, tm, d), lambda b, i: (b, i, 0)),
        scratch_shapes=[pltpu.VMEM((tm + 2 * POOL_HALO, d), F32)],
        compiler_params=_cparams(2), name="pool_mix",
    )(x, x, x, mod, g.reshape(1, d), pool_w, pool_scale.reshape(1, d))


def _rope_table_kernel(inv_ref, cos_ref, sin_ref):
    tm = cos_ref.shape[0]
    t = pl.program_id(0) * tm + lax.broadcasted_iota(jnp.int32, (tm, LANE), 0)
    lane = lax.broadcasted_iota(jnp.int32, (tm, LANE), 1)
    pos = jnp.where((lane & 63) < 32, lax.shift_right_logical(t, 6), t & (GRID_W - 1)).astype(F32)
    ang = pos * inv_ref[...]
    cos_ref[...] = jnp.cos(ang)
    sin_ref[...] = jnp.where((lane & 31) < 16, -jnp.sin(ang), jnp.sin(ang))


def _rope_tables(seq):
    per_axis = DIFF_HD // 2
    inv = ROPE_THETA ** (-jnp.arange(0, per_axis, 2, dtype=F32) / per_axis)
    inv_lane = jnp.tile(inv, LANE // inv.shape[0]).reshape(1, LANE)
    tm = 1024
    return pl.pallas_call(
        _rope_table_kernel,
        out_shape=(jax.ShapeDtypeStruct((seq, LANE), F32),) * 2,
        grid=(seq // tm,),
        in_specs=[pl.BlockSpec((1, LANE), lambda i: (0, 0))],
        out_specs=(pl.BlockSpec((tm, LANE), lambda i: (i, 0)),) * 2,
        compiler_params=_cparams(1), name="rope_tables",
    )(inv_lane)


def _qkv_kernel(x_ref, mod_ref, g_ref, wqk_ref, wvt_ref, *rest, rope, bias, scale):
    rest = list(rest)
    if bias:
        bqk_ref, bvt_ref = rest[:2]
        rest = rest[2:]
    if rope:
        cos_ref, sin_ref = rest[:2]
        rest = rest[2:]
    q_ref, k_ref, vt_ref = rest
    d = x_ref.shape[2]
    h = _norm_mod(x_ref[0], g_ref[...], mod_ref, 3).astype(BF16)
    vt = lax.dot_general(wvt_ref[...], h, (((1,), (1,)), ((), ())), preferred_element_type=F32)
    if bias:
        vt = vt + bvt_ref[...]
    vt_ref[0] = vt.astype(BF16)
    if rope:
        cos = cos_ref[...]
        sin = sin_ref[...]
        lane = lax.broadcasted_iota(jnp.int32, cos.shape, 1)
        first_half = (lane & 31) < 16
    for cb in range(2 * d // FF_CHUNK):
        c0 = cb * FF_CHUNK
        t2 = jnp.dot(h, wqk_ref[:, c0:c0 + FF_CHUNK], preferred_element_type=F32)
        if bias:
            t2 = t2 + bqk_ref[:, c0:c0 + FF_CHUNK]
        for half in range(FF_CHUNK // LANE):
            t = t2[:, half * LANE:(half + 1) * LANE]
            if rope:
                partner = jnp.where(first_half, pltpu.roll(t, LANE - 16, 1), pltpu.roll(t, 16, 1))
                t = t * cos + partner * sin
            col = c0 + half * LANE
            if col < d:
                q_ref[0, :, col:col + LANE] = (t * scale).astype(BF16)
            else:
                k_ref[0, :, col - d:col - d + LANE] = t.astype(BF16)


def _qkv(x, mod, g, w_qkv, scale, b_qkv=None, rope_tabs=None):
    bx, sx, d = x.shape
    tm = min(512, sx)
    wqk = w_qkv[:, :2 * d].astype(BF16)
    wvt = w_qkv[:, 2 * d:].T.astype(BF16)
    in_specs = [pl.BlockSpec((1, tm, d), lambda b, i: (b, i, 0)), _mod_spec(mod), _const_spec((1, d)),
                _const_spec((d, 2 * d)), _const_spec((d, d))]
    args = [x, mod, g.reshape(1, d), wqk, wvt]
    if b_qkv is not None:
        in_specs += [_const_spec((1, 2 * d)), _const_spec((d, 1))]
        args += [b_qkv[:2 * d].reshape(1, 2 * d), b_qkv[2 * d:].reshape(d, 1)]
    if rope_tabs is not None:
        in_specs += [pl.BlockSpec((tm, LANE), lambda b, i: (i, 0))] * 2
        args += list(rope_tabs)
    tok = pl.BlockSpec((1, tm, d), lambda b, i: (b, i, 0))
    return pl.pallas_call(
        functools.partial(_qkv_kernel, rope=rope_tabs is not None, bias=b_qkv is not None, scale=scale),
        out_shape=(jax.ShapeDtypeStruct((bx, sx, d), BF16), jax.ShapeDtypeStruct((bx, sx, d), BF16),
                   jax.ShapeDtypeStruct((bx, d, sx), BF16)),
        grid=(bx, sx // tm),
        in_specs=in_specs,
        out_specs=(tok, tok, pl.BlockSpec((1, d, tm), lambda b, i: (b, 0, i))),
        compiler_params=_cparams(2), name="qkv_proj",
    )(*args)


def _diff_attn_kernel(lam_ref, g_ref, q_ref, kc_ref, vct_ref, *rest, lam_init, tk):
    if len(rest) == 6:
        k_ref, vt_ref, o_ref, acc_ref, sa_ref, sb_ref = rest
    else:
        k_ref = vt_ref = None
        o_ref, acc_ref = rest
    tq = q_ref.shape[1]
    q = q_ref[0]
    lane = lax.broadcasted_iota(jnp.int32, q.shape, 1)
    zero = jnp.zeros_like(q)
    qq = jnp.concatenate([jnp.where(lane < DIFF_HD, q, zero), jnp.where(lane >= DIFF_HD, q, zero)], axis=0)

    def scores(k_t):
        return lax.dot_general(k_t, qq, (((1,), (1,)), ((), ())), preferred_element_type=F32)

    s = scores(kc_ref[0])
    m = jnp.max(s, axis=0, keepdims=True)
    p = jnp.exp2(s - m)
    l = jnp.sum(p, axis=0, keepdims=True)
    acc_ref[...] = jnp.dot(vct_ref[0], p.astype(BF16), preferred_element_type=F32)

    if k_ref is not None:
        n_chunks = k_ref.shape[1] // tk

        def chunk_scores(c):
            return scores(k_ref[0, c * tk:(c + 1) * tk, :])

        def online(s_ref, c, m, l):
            s = s_ref[...]
            m_new = jnp.maximum(m, jnp.max(s, axis=0, keepdims=True))
            alpha = jnp.exp2(m - m_new)
            p = jnp.exp2(s - m_new)
            l = alpha * l + jnp.sum(p, axis=0, keepdims=True)
            vt_t = vt_ref[0, :, c * tk:(c + 1) * tk]
            acc_ref[...] = alpha * acc_ref[...] + jnp.dot(vt_t, p.astype(BF16), preferred_element_type=F32)
            return m_new, l

        bufs = (sa_ref, sb_ref)
        bufs[0][...] = chunk_scores(0)
        for c in range(n_chunks):
            if c + 1 < n_chunks:
                bufs[(c + 1) % 2][...] = chunk_scores(c + 1)
            m, l = online(bufs[c % 2], c, m, l)

    lf = lam_ref[...]
    lam = (jnp.exp(jnp.sum(lf[0:1] * lf[1:2], axis=-1, keepdims=True))
           - jnp.exp(jnp.sum(lf[2:3] * lf[3:4], axis=-1, keepdims=True)) + lam_init)
    acc = acc_ref[...]
    inv = 1.0 / l
    o = acc[:, :tq] * inv[:, :tq] - lam * (acc[:, tq:] * inv[:, tq:])
    o = o * lax.rsqrt(jnp.mean(o * o, axis=0, keepdims=True) + NORM_EPS)
    o = (o * g_ref[...]) * (1.0 - lam_init)
    o_ref[0] = o.T.astype(BF16)


def _diff_attn(lam, subln_g, q, kc, vct, k, vt, lam_init):
    bq, sq, d = q.shape
    n_ctx = kc.shape[1]
    tq = 256
    hw = 2 * DIFF_HD
    in_specs = [_const_spec((4, DIFF_HD)), _const_spec((hw, 1)),
                pl.BlockSpec((1, tq, hw), lambda b, h, i: (b, i, h)),
                pl.BlockSpec((1, n_ctx, hw), lambda b, h, i: (b, 0, h)),
                pl.BlockSpec((1, hw, n_ctx), lambda b, h, i: (b, h, 0))]
    args = [lam, subln_g.reshape(hw, 1), q, kc, vct]
    scratch = [pltpu.VMEM((hw, 2 * tq), F32)]
    tk = 1024
    if k is not None:
        s = k.shape[1]
        assert s % tk == 0
        in_specs += [pl.BlockSpec((1, s, hw), lambda b, h, i: (b, 0, h)),
                     pl.BlockSpec((1, hw, s), lambda b, h, i: (b, h, 0))]
        args += [k, vt]
        scratch += [pltpu.VMEM((tk, 2 * tq), F32)] * 2
    return pl.pallas_call(
        functools.partial(_diff_attn_kernel, lam_init=lam_init, tk=tk),
        out_shape=jax.ShapeDtypeStruct((bq, sq, d), BF16),
        grid=(bq, DIFF_HEADS, sq // tq),
        in_specs=in_specs,
        out_specs=pl.BlockSpec((1, tq, hw), lambda b, h, i: (b, i, h)),
        scratch_shapes=scratch,
        compiler_params=_cparams(3), name="diff_attn",
    )(*args)


def _nat_bias_rows():
    table = []
    for v, shift in enumerate((0, 4, 8)):
        rows = []
        for i in range(NAT_BAND):
            per_d = []
            for dq in range(NAT_QROWS):
                first = (0, dq, 4)[v]
                ok = first <= i < first + NAT_WIN_ROWS
                per_d.append(i - shift - dq + NAT_WIN_ROWS - 1 if ok else None)
            rows.append(per_d)
        table.append(rows)
    return table


def _nat_toeplitz_kernel(r_ref, o_ref):
    m, n = o_ref.shape

    def window(shape):
        col = lax.broadcasted_iota(jnp.int32, shape, 1)
        kc = lax.shift_right_logical(col, 7)
        qc = col & (GRID_W - 1)
        cs = jnp.clip(qc - NAT_WIN_COLS // 2, 0, GRID_W - NAT_WIN_COLS)
        return jnp.where(kc >= cs, jnp.where(kc < cs + NAT_WIN_COLS, kc - qc + NAT_WIN_COLS - 1, -1), -1)

    j = lax.broadcasted_iota(jnp.int32, (LANE, n), 0)
    onehot = jnp.where(j == window((LANE, n)), 1.0, 0.0).astype(BF16)
    r = r_ref[...]
    r_hi = r.astype(BF16)
    r1 = r - r_hi.astype(F32)
    r_mid = r1.astype(BF16)
    r_lo = (r1 - r_mid.astype(F32)).astype(BF16)
    t = (jnp.dot(r_hi, onehot, preferred_element_type=F32) + jnp.dot(r_mid, onehot, preferred_element_type=F32)
         + jnp.dot(r_lo, onehot, preferred_element_type=F32))
    o_ref[...] = jnp.where(window((m, n)) >= 0, t * LOG2_E, NEG)


def _nat_bias_kernel(t_ref, o_ref):
    lane = lax.broadcasted_iota(jnp.int32, (GRID_W, LANE), 1)
    neg = jnp.full((GRID_W, LANE), NEG, F32)
    rows = _nat_bias_rows()
    for v in range(3):
        for i in range(NAT_BAND):
            for pair in range(NAT_QROWS // 2):
                a_l, a_r = rows[v][i][2 * pair], rows[v][i][2 * pair + 1]
                left = neg if a_l is None else t_ref[0, a_l]
                right = neg if a_r is None else t_ref[0, a_r]
                o_ref[0, v, i * GRID_W:(i + 1) * GRID_W, pair * LANE:(pair + 1) * LANE] = (
                    jnp.where(lane < GRID_W, left, right))


def _nat_bias(rpb):
    nh, nr, nc = rpb.shape
    r2 = jnp.pad(rpb, ((0, 0), (0, 16 - nr), (0, LANE - nc))).reshape(nh * 16, LANE)
    n = GRID_W * LANE
    t = pl.pallas_call(
        _nat_toeplitz_kernel,
        out_shape=jax.ShapeDtypeStruct((nh * 16, n), F32),
        grid=(1,),
        in_specs=[pl.BlockSpec((nh * 16, LANE), lambda i: (0, 0))],
        out_specs=pl.BlockSpec((nh * 16, n), lambda i: (0, 0)),
        compiler_params=_cparams(1), name="nat_toeplitz",
    )(r2)
    t = t.reshape(nh, 16, GRID_W, LANE)
    nk, nq = NAT_BAND * GRID_W, NAT_QROWS * GRID_W
    return pl.pallas_call(
        _nat_bias_kernel,
        out_shape=jax.ShapeDtypeStruct((nh, 3, nk, nq), F32),
        grid=(nh,),
        in_specs=[pl.BlockSpec((1, 16, GRID_W, LANE), lambda h: (h, 0, 0, 0))],
        out_specs=pl.BlockSpec((1, 3, nk, nq), lambda h: (h, 0, 0, 0)),
        compiler_params=_cparams(1), name="nat_bias",
    )(t)


def _nat_attn_kernel(q_ref, kc_ref, vct_ref, *rest, n_rows, n_blk, group):
    if len(rest) == 4:
        bias_ref, k_ref, vt_ref, o_ref = rest
    else:
        bias_ref = k_ref = vt_ref = None
        (o_ref,) = rest
    tq = NAT_QROWS * GRID_W
    nk = NAT_BAND * GRID_W
    dn = (((1,), (1,)), ((), ()))
    lane = lax.broadcasted_iota(jnp.int32, (tq, LANE), 1)
    zero = jnp.zeros((tq, LANE), BF16)
    for g in range(group):
        q = q_ref[0, g * tq:(g + 1) * tq, :]
        if k_ref is not None:
            blk = pl.program_id(2) * group + g
            start = jnp.clip(blk * NAT_QROWS - NAT_WIN_ROWS // 2, 0, n_rows - NAT_BAND)
            off = pl.multiple_of(start * GRID_W, NAT_QROWS * GRID_W)
            kb = k_ref[0, pl.ds(off, nk), :]
            variant = jnp.where(blk == 0, 0, jnp.where(blk == n_blk - 1, 2, 1))
        outs = []
        for hh in range(LANE // NAT_HD):
            qh = jnp.where((lane >= hh * NAT_HD) & (lane < (hh + 1) * NAT_HD), q, zero)
            rows = slice(hh * NAT_HD, (hh + 1) * NAT_HD)
            s_c = lax.dot_general(kc_ref[0], qh, dn, preferred_element_type=F32)
            m = jnp.max(s_c, axis=0, keepdims=True)
            if k_ref is not None:
                s_b = lax.dot_general(kb, qh, dn, preferred_element_type=F32) + bias_ref[hh, variant]
                m = jnp.maximum(m, jnp.max(s_b, axis=0, keepdims=True))
            p_c = jnp.exp2(s_c - m)
            l = jnp.sum(p_c, axis=0, keepdims=True)
            o = jnp.dot(vct_ref[0, rows, :], p_c.astype(BF16), preferred_element_type=F32)
            if k_ref is not None:
                p_b = jnp.exp2(s_b - m)
                l = l + jnp.sum(p_b, axis=0, keepdims=True)
                o = o + jnp.dot(vt_ref[0, rows, pl.ds(off, nk)], p_b.astype(BF16), preferred_element_type=F32)
            outs.append(o / l)
        o_ref[0, g * tq:(g + 1) * tq, :] = jnp.concatenate(outs, axis=0).T.astype(BF16)


def _nat_attn(q, kc, vct, bias=None, k=None, vt=None):
    bq, sq, d = q.shape
    n_ctx = kc.shape[1]
    tq = NAT_QROWS * GRID_W
    n_blk = sq // tq
    group = math.gcd(n_blk, 4)
    in_specs = [pl.BlockSpec((1, group * tq, LANE), lambda hp, b, i: (b, i, hp)),
                pl.BlockSpec((1, n_ctx, LANE), lambda hp, b, i: (b, 0, hp)),
                pl.BlockSpec((1, LANE, n_ctx), lambda hp, b, i: (b, hp, 0))]
    args = [q, kc, vct]
    n_rows = None
    if k is not None:
        s = k.shape[1]
        n_rows = s // GRID_W
        nk = NAT_BAND * GRID_W
        in_specs += [pl.BlockSpec((LANE // NAT_HD, 3, nk, tq), lambda hp, b, i: (hp, 0, 0, 0)),
                     pl.BlockSpec((1, s, LANE), lambda hp, b, i: (b, 0, hp)),
                     pl.BlockSpec((1, LANE, s), lambda hp, b, i: (b, hp, 0))]
        args += [bias, k, vt]
    return pl.pallas_call(
        functools.partial(_nat_attn_kernel, n_rows=n_rows, n_blk=n_blk, group=group),
        out_shape=jax.ShapeDtypeStruct((bq, sq, d), BF16),
        grid=(d // LANE, bq, n_blk // group),
        in_specs=in_specs,
        out_specs=pl.BlockSpec((1, group * tq, LANE), lambda hp, b, i: (b, i, hp)),
        compiler_params=_cparams(3), name="nat_attn",
    )(*args)


def _out_proj_kernel(x_ref, o_ref_in, mod_ref, w_ref, *rest, bias):
    if bias:
        b_ref, out_ref = rest
    else:
        (out_ref,) = rest
    y = jnp.dot(o_ref_in[0], w_ref[...], preferred_element_type=F32)
    if bias:
        y = y + b_ref[...]
    out_ref[0] = x_ref[0] + mod_ref[0, 5:6, :] * y


def _out_proj(x, o, mod, w_o, b_o=None):
    bx, sx, d = x.shape
    tm = min(512, sx)
    tok = pl.BlockSpec((1, tm, d), lambda b, i: (b, i, 0))
    in_specs = [tok, tok, _mod_spec(mod), _const_spec((d, d))]
    args = [x, o, mod, w_o.astype(BF16)]
    if b_o is not None:
        in_specs.append(_const_spec((1, d)))
        args.append(b_o.reshape(1, d))
    return pl.pallas_call(
        functools.partial(_out_proj_kernel, bias=b_o is not None),
        out_shape=jax.ShapeDtypeStruct(x.shape, F32),
        grid=(bx, sx // tm),
        in_specs=in_specs, out_specs=tok,
        compiler_params=_cparams(2), name="out_proj",
    )(*args)


def kernel(x, c, ctx, c_ctx, ada_w, ada_b, norm_g, ffn_w_in, ffn_w_out, pool_w, pool_scale,
           diff_w_qkv, diff_lam, diff_subln_g, diff_w_o, nat_w_qkv, nat_b_qkv, nat_rpb,
           nat_w_o, nat_b_o, final_g):
    b, seq, d = x.shape
    mods = _ada_mods(c, c_ctx, ada_w, ada_b)
    xc = ctx
    for i in range(DEPTH):
        kind = i % N_MIXERS
        j = i // N_MIXERS
        last = i == DEPTH - 1
        update_ctx = not last
        ctx_needed = update_ctx or kind != 0
        mx = mods[i, :b].reshape(b, N_MOD, d)
        mc = mods[i, b:b + 1].reshape(1, N_MOD, d)
        w_in = ffn_w_in[i].astype(BF16)
        w_out = ffn_w_out[i].astype(BF16)

        x = _ffn(x, mx, norm_g[i, 0], w_in[0], w_out[0], 0)
        if ctx_needed:
            xc = _ffn(xc, mc, norm_g[i, 0], w_in[0], w_out[0], 0)

        if kind == 0:
            pw = pool_w[j].astype(BF16)
            x = _pool(x, mx, norm_g[i, 1], pw, pool_scale[j])
            if update_ctx:
                xc = _pool(xc, mc, norm_g[i, 1], pw, pool_scale[j])
        elif kind == 1:
            lam_init = 0.8 - 0.6 * math.exp(-0.3 * i)
            scale = DIFF_HD ** -0.5 * LOG2_E
            q_l, k_l, vt_l = _qkv(x, mx, norm_g[i, 1], diff_w_qkv[j], scale, rope_tabs=_rope_tables(seq))
            q_c, k_c, vt_c = _qkv(xc, mc, norm_g[i, 1], diff_w_qkv[j], scale)
            o_l = _diff_attn(diff_lam[j], diff_subln_g[j], q_l, k_c, vt_c, k_l, vt_l, lam_init)
            x = _out_proj(x, o_l, mx, diff_w_o[j])
            if update_ctx:
                o_c = _diff_attn(diff_lam[j], diff_subln_g[j], q_c, k_c, vt_c, None, None, lam_init)
                xc = _out_proj(xc, o_c, mc, diff_w_o[j])
        else:
            scale = NAT_HD ** -0.5 * LOG2_E
            q_l, k_l, vt_l = _qkv(x, mx, norm_g[i, 1], nat_w_qkv[j], scale, b_qkv=nat_b_qkv[j])
            q_c, k_c, vt_c = _qkv(xc, mc, norm_g[i, 1], nat_w_qkv[j], scale, b_qkv=nat_b_qkv[j])
            o_l = _nat_attn(q_l, k_c, vt_c, _nat_bias(nat_rpb[j]), k_l, vt_l)
            x = _out_proj(x, o_l, mx, nat_w_o[j], nat_b_o[j])
            if update_ctx:
                o_c = _nat_attn(q_c, k_c, vt_c)
                xc = _out_proj(xc, o_c, mc, nat_w_o[j], nat_b_o[j])

        x = _ffn(x, mx, norm_g[i, 2], w_in[1], w_out[1], 6, final_g=final_g if last else None)
        if update_ctx:
            xc = _ffn(xc, mc, norm_g[i, 2], w_in[1], w_out[1], 6)
    return x
```

```python
import functools
import math

import numpy as np
import jax
import jax.numpy as jnp
from jax import lax
from jax.experimental import pallas as pl
from jax.experimental.pallas import tpu as pltpu

D_MODEL = 1024
DEPTH = 4
GRID_W = 64
N_MIXERS = 3
D_FF = 2816
N_MOD = 9
POOL_WINDOWS = (2, 4, 8, 16)
POOL_GW = 256
POOL_HALO = 8
DIFF_HEADS = 8
DIFF_HD = 64
NAT_HEADS = 16
NAT_HD = 64
NAT_WIN_ROWS = 8
NAT_WIN_COLS = 16
NAT_QROWS = 4
NAT_BAND = 12
ROPE_THETA = 10000.0
NORM_EPS = 1e-6
NEG = -1e30
LOG2_E = math.log2(math.e)
BOUND_MARGIN = 1.01
FIXED_OFFSET_SPAN = 64.0

LANE = 128
FF_CHUNK = 256
VMEM_LIMIT = 56 * 1024 * 1024

F32 = jnp.float32
BF16 = jnp.bfloat16


def _cparams(n_axes):
    return pltpu.CompilerParams(dimension_semantics=("arbitrary",) * n_axes, vmem_limit_bytes=VMEM_LIMIT)


def _const_spec(shape):
    nd = len(shape)
    return pl.BlockSpec(shape, lambda *_: (0,) * nd, pipeline_mode=pl.Buffered(1))


def _mod_spec(mod):
    if mod.shape[0] == 1:
        return pl.BlockSpec((1, N_MOD, D_MODEL), lambda b, i: (0, 0, 0))
    return pl.BlockSpec((1, N_MOD, D_MODEL), lambda b, i: (b, 0, 0))


def _sigmoid(x):
    return 1.0 / (1.0 + jnp.exp(-x))


def _rms(x):
    return x * lax.rsqrt(jnp.mean(x * x, axis=-1, keepdims=True) + NORM_EPS)


def _norm_mod(x, g, mod_ref, row):
    shift = mod_ref[0, row:row + 1, :]
    scale = mod_ref[0, row + 1:row + 2, :]
    return (_rms(x) * g) * (1.0 + scale) + shift


def _ada_kernel(c_ref, w_ref, b_ref, o_ref):
    c = c_ref[...]
    s = (c * _sigmoid(c)).astype(BF16)
    o_ref[0] = jnp.dot(s, w_ref[0].astype(BF16), preferred_element_type=F32) + b_ref[0]


def _ada_mods(c, c_ctx, ada_w, ada_b):
    d = D_MODEL
    cc = jnp.concatenate([c, c_ctx[None], jnp.zeros((3, d), F32)], axis=0)
    tn = 1024
    return pl.pallas_call(
        _ada_kernel,
        out_shape=jax.ShapeDtypeStruct((DEPTH, 8, N_MOD * d), F32),
        grid=(DEPTH, N_MOD * d // tn),
        in_specs=[pl.BlockSpec((8, d), lambda l, n: (0, 0)),
                  pl.BlockSpec((1, d, tn), lambda l, n: (l, 0, n)),
                  pl.BlockSpec((1, 1, tn), lambda l, n: (l, 0, n))],
        out_specs=pl.BlockSpec((1, 8, tn), lambda l, n: (l, 0, n)),
        compiler_params=_cparams(2), name="ada_mods",
    )(cc, ada_w, ada_b.reshape(DEPTH, 1, N_MOD * d))


def _swiglu_half_step(x, mod_ref, g_ref, win_ref, wout_ref, a_ref, row):
    h = _norm_mod(x, g_ref[...], mod_ref, row).astype(BF16)
    for j in range(D_FF // FF_CHUNK):
        lo = j * FF_CHUNK
        g = jnp.dot(h, win_ref[:, lo:lo + FF_CHUNK], preferred_element_type=F32)
        u = jnp.dot(h, win_ref[:, D_FF + lo:D_FF + lo + FF_CHUNK], preferred_element_type=F32)
        a_ref[:, lo:lo + FF_CHUNK] = ((g * _sigmoid(g)) * u).astype(BF16)
    y = jnp.dot(a_ref[...], wout_ref[...], preferred_element_type=F32)
    return x + (0.5 * mod_ref[0, row + 2:row + 3, :]) * y


def _pool_mix(x_ref, xp_ref, xn_ref, mod_ref, g_ref, pw_ref, ps_ref, ext_ref, seq):
    i = pl.program_id(1)
    last = pl.num_programs(1) - 1
    tm = x_ref.shape[1]
    g = g_ref[...]
    x = x_ref[0]
    h = _norm_mod(x, g, mod_ref, 3)
    hp = _norm_mod(xp_ref[0], g, mod_ref, 3)
    hn = _norm_mod(xn_ref[0], g, mod_ref, 3)
    ext_ref[0:POOL_HALO, :] = jnp.where(i > 0, hp, 0.0)
    ext_ref[POOL_HALO:POOL_HALO + tm, :] = h
    ext_ref[POOL_HALO + tm:, :] = jnp.where(i < last, hn, 0.0)
    t = i * tm + lax.broadcasted_iota(jnp.int32, (tm, 1), 0)
    gate = mod_ref[0, 5:6, :]
    groups = []
    for gi, win in enumerate(POOL_WINDOWS):
        c0 = gi * POOL_GW
        lo_off = -(win // 2)
        acc = ext_ref[POOL_HALO + lo_off:POOL_HALO + lo_off + tm, c0:c0 + POOL_GW]
        for o in range(lo_off + 1, lo_off + win):
            acc = acc + ext_ref[POOL_HALO + o:POOL_HALO + o + tm, c0:c0 + POOL_GW]
        lo = jnp.clip(t + lo_off, 0, seq - 1)
        hi = jnp.clip(t + lo_off + win - 1, 0, seq - 1)
        cnt = (hi - lo + 1).astype(F32)
        diff = (acc / cnt - h[:, c0:c0 + POOL_GW]).astype(BF16)
        y = jnp.dot(diff, pw_ref[gi], preferred_element_type=F32) * ps_ref[:, c0:c0 + POOL_GW]
        groups.append(x[:, c0:c0 + POOL_GW] + gate[:, c0:c0 + POOL_GW] * y)
    return jnp.concatenate(groups, axis=1)


def _ffn_kernel(*refs, mixer, seq, bias, final):
    refs = list(refs)
    x_ref = refs.pop(0)
    if mixer == "pool":
        xp_ref, xn_ref = refs.pop(0), refs.pop(0)
    elif mixer == "proj":
        attn_ref = refs.pop(0)
    mod_ref = refs.pop(0)
    if mixer == "pool":
        g1_ref, pw_ref, ps_ref = refs.pop(0), refs.pop(0), refs.pop(0)
    elif mixer == "proj":
        wo_ref = refs.pop(0)
        bo_ref = refs.pop(0) if bias else None
    g_ref, win_ref, wout_ref = refs.pop(0), refs.pop(0), refs.pop(0)
    fg_ref = refs.pop(0) if final else None
    o_ref, a_ref = refs.pop(0), refs.pop(0)
    if mixer == "pool":
        x = _pool_mix(x_ref, xp_ref, xn_ref, mod_ref, g1_ref, pw_ref, ps_ref, refs.pop(0), seq)
        row = 6
    elif mixer == "proj":
        y = jnp.dot(attn_ref[0], wo_ref[...], preferred_element_type=F32)
        if bias:
            y = y + bo_ref[...]
        x = x_ref[0] + mod_ref[0, 5:6, :] * y
        row = 6
    else:
        x = x_ref[0]
        row = 0
    out = _swiglu_half_step(x, mod_ref, g_ref, win_ref, wout_ref, a_ref, row)
    if final:
        out = _rms(out) * fg_ref[...]
    o_ref[0] = out


def _ffn(x, mod, g, w_in, w_out, pool=None, proj=None, final_g=None):
    bx, sx, d = x.shape
    tm = min(512, sx)
    tok = pl.BlockSpec((1, tm, d), lambda b, i: (b, i, 0))
    in_specs, args, scratch = [tok], [x], [pltpu.VMEM((tm, D_FF), BF16)]
    mixer = None
    if pool is not None:
        mixer = "pool"
        nb, n8 = tm // POOL_HALO, sx // POOL_HALO
        in_specs += [pl.BlockSpec((1, POOL_HALO, d), lambda b, i: (b, jnp.maximum(i * nb - 1, 0), 0)),
                     pl.BlockSpec((1, POOL_HALO, d), lambda b, i: (b, jnp.minimum((i + 1) * nb, n8 - 1), 0))]
        args += [x, x]
        scratch.append(pltpu.VMEM((tm + 2 * POOL_HALO, d), F32))
    elif proj is not None:
        mixer = "proj"
        in_specs.append(tok)
        args.append(proj[0])
    in_specs.append(_mod_spec(mod))
    args.append(mod)
    if pool is not None:
        in_specs += [_const_spec((1, d)), _const_spec((len(POOL_WINDOWS), POOL_GW, POOL_GW)), _const_spec((1, d))]
        args += [pool[0].reshape(1, d), pool[1], pool[2].reshape(1, d)]
    elif proj is not None:
        in_specs.append(_const_spec((d, d)))
        args.append(proj[1].astype(BF16))
        if proj[2] is not None:
            in_specs.append(_const_spec((1, d)))
            args.append(proj[2].reshape(1, d))
    in_specs += [_const_spec((1, d)), _const_spec((d, 2 * D_FF)), _const_spec((D_FF, d))]
    args += [g.reshape(1, d), w_in, w_out]
    if final_g is not None:
        in_specs.append(_const_spec((1, d)))
        args.append(final_g.reshape(1, d))
    return pl.pallas_call(
        functools.partial(_ffn_kernel, mixer=mixer, seq=sx, bias=proj is not None and proj[2] is not None,
                          final=final_g is not None),
        out_shape=jax.ShapeDtypeStruct(x.shape, F32),
        grid=(bx, sx // tm),
        in_specs=in_specs,
        out_specs=tok,
        scratch_shapes=scratch,
        compiler_params=_cparams(2), name="ffn" if mixer is None else mixer + "_ffn",
    )(*args)


def _rope_table_kernel(inv_ref, cos_ref, sin_ref):
    tm = cos_ref.shape[0]
    t = pl.program_id(0) * tm + lax.broadcasted_iota(jnp.int32, (tm, LANE), 0)
    lane = lax.broadcasted_iota(jnp.int32, (tm, LANE), 1)
    pos = jnp.where((lane & 63) < 32, lax.shift_right_logical(t, 6), t & (GRID_W - 1)).astype(F32)
    ang = pos * inv_ref[...]
    cos_ref[...] = jnp.cos(ang)
    sin_ref[...] = jnp.where((lane & 31) < 16, -jnp.sin(ang), jnp.sin(ang))


def _rope_tables(seq):
    per_axis = DIFF_HD // 2
    inv = ROPE_THETA ** (-jnp.arange(0, per_axis, 2, dtype=F32) / per_axis)
    inv_lane = jnp.tile(inv, LANE // inv.shape[0]).reshape(1, LANE)
    tm = 1024
    return pl.pallas_call(
        _rope_table_kernel,
        out_shape=(jax.ShapeDtypeStruct((seq, LANE), F32),) * 2,
        grid=(seq // tm,),
        in_specs=[pl.BlockSpec((1, LANE), lambda i: (0, 0))],
        out_specs=(pl.BlockSpec((tm, LANE), lambda i: (i, 0)),) * 2,
        compiler_params=_cparams(1), name="rope_tables",
    )(inv_lane)


def _qkv_kernel(x_ref, mod_ref, g_ref, wqk_ref, wvt_ref, *rest, rope, bias, scale):
    rest = list(rest)
    if bias:
        bqk_ref, bvt_ref = rest[:2]
        rest = rest[2:]
    if rope:
        cos_ref, sin_ref = rest[:2]
        rest = rest[2:]
    q_ref, k_ref, vt_ref = rest
    d = x_ref.shape[2]
    h = _norm_mod(x_ref[0], g_ref[...], mod_ref, 3).astype(BF16)
    vt = lax.dot_general(wvt_ref[...], h, (((1,), (1,)), ((), ())), preferred_element_type=F32)
    if bias:
        vt = vt + bvt_ref[...]
    vt_ref[0] = vt.astype(BF16)
    if rope:
        cos = cos_ref[...]
        sin = sin_ref[...]
        lane = lax.broadcasted_iota(jnp.int32, cos.shape, 1)
        first_half = (lane & 31) < 16
    for cb in range(2 * d // FF_CHUNK):
        c0 = cb * FF_CHUNK
        t2 = jnp.dot(h, wqk_ref[:, c0:c0 + FF_CHUNK], preferred_element_type=F32)
        if bias:
            t2 = t2 + bqk_ref[:, c0:c0 + FF_CHUNK]
        for half in range(FF_CHUNK // LANE):
            t = t2[:, half * LANE:(half + 1) * LANE]
            if rope:
                partner = jnp.where(first_half, pltpu.roll(t, LANE - 16, 1), pltpu.roll(t, 16, 1))
                t = t * cos + partner * sin
            col = c0 + half * LANE
            if col < d:
                q_ref[0, :, col:col + LANE] = (t * scale).astype(BF16)
            else:
                k_ref[0, :, col - d:col - d + LANE] = t.astype(BF16)


def _qkv(x, mod, g, w_qkv, scale, b_qkv=None, rope_tabs=None):
    bx, sx, d = x.shape
    tm = min(512, sx)
    wqk = w_qkv[:, :2 * d].astype(BF16)
    wvt = w_qkv[:, 2 * d:].T.astype(BF16)
    in_specs = [pl.BlockSpec((1, tm, d), lambda b, i: (b, i, 0)), _mod_spec(mod), _const_spec((1, d)),
                _const_spec((d, 2 * d)), _const_spec((d, d))]
    args = [x, mod, g.reshape(1, d), wqk, wvt]
    if b_qkv is not None:
        in_specs += [_const_spec((1, 2 * d)), _const_spec((d, 1))]
        args += [b_qkv[:2 * d].reshape(1, 2 * d), b_qkv[2 * d:].reshape(d, 1)]
    if rope_tabs is not None:
        in_specs += [pl.BlockSpec((tm, LANE), lambda b, i: (i, 0))] * 2
        args += list(rope_tabs)
    tok = pl.BlockSpec((1, tm, d), lambda b, i: (b, i, 0))
    return pl.pallas_call(
        functools.partial(_qkv_kernel, rope=rope_tabs is not None, bias=b_qkv is not None, scale=scale),
        out_shape=(jax.ShapeDtypeStruct((bx, sx, d), BF16), jax.ShapeDtypeStruct((bx, sx, d), BF16),
                   jax.ShapeDtypeStruct((bx, d, sx), BF16)),
        grid=(bx, sx // tm),
        in_specs=in_specs,
        out_specs=(tok, tok, pl.BlockSpec((1, d, tm), lambda b, i: (b, 0, i))),
        compiler_params=_cparams(2), name="qkv_proj",
    )(*args)


def _subhead_sq_norms(x, ones):
    xf = x.astype(F32)
    return jnp.dot((xf * xf).astype(BF16), ones, preferred_element_type=F32)


def _diff_attn_kernel(lam_ref, g_ref, q_ref, kc_ref, vct_ref, *rest, lam_init, tk):
    if len(rest) == 9:
        k_ref, vt_ref, o_ref, acc_ref, sa_ref, sb_ref, sc_ref, ml_ref, kn_ref = rest
    else:
        k_ref = vt_ref = None
        o_ref, acc_ref = rest
    tq = q_ref.shape[1]
    q = q_ref[0]
    lane = lax.broadcasted_iota(jnp.int32, q.shape, 1)
    zero = jnp.zeros_like(q)
    qq = jnp.concatenate([jnp.where(lane < DIFF_HD, q, zero), jnp.where(lane >= DIFF_HD, q, zero)], axis=0)

    def scores(k_t):
        return lax.dot_general(k_t, qq, (((1,), (1,)), ((), ())), preferred_element_type=F32)

    s = scores(kc_ref[0])
    m = jnp.max(s, axis=0, keepdims=True)

    def context_block(s, m):
        p = jnp.exp2(s - m)
        acc_ref[...] = jnp.dot(vct_ref[0], p.astype(BF16), preferred_element_type=F32)
        return jnp.sum(p, axis=0, keepdims=True)

    if k_ref is None:
        l = context_block(s, m)
    else:
        n_chunks = k_ref.shape[1] // tk
        hw = 2 * DIFF_HD

        @pl.when(pl.program_id(2) == 0)
        def _():
            r = lax.broadcasted_iota(jnp.int32, (hw, hw), 0)
            c_ = lax.broadcasted_iota(jnp.int32, (hw, hw), 1)
            same_subhead = jnp.where((r < DIFF_HD) == (c_ < DIFF_HD), 1.0, 0.0).astype(BF16)
            kn = jnp.max(_subhead_sq_norms(kc_ref[0], same_subhead), axis=0, keepdims=True)
            for c in range(n_chunks):
                kn = jnp.maximum(kn, jnp.max(_subhead_sq_norms(k_ref[0, c * tk:(c + 1) * tk, :], same_subhead),
                                             axis=0, keepdims=True))
            kn_ref[...] = kn

        r = lax.broadcasted_iota(jnp.int32, (16, hw), 0)
        c_ = lax.broadcasted_iota(jnp.int32, (16, hw), 1)
        select = jnp.where(r == jnp.where(c_ < DIFF_HD, 0, 1), 1.0, 0.0).astype(BF16)
        qf = q.astype(F32)
        qn = lax.dot_general(select, (qf * qf).astype(BF16), (((1,), (1,)), ((), ())),
                             preferred_element_type=F32)
        kn = kn_ref[...]
        bound = jnp.concatenate([jnp.sqrt(qn[0:1, :] * kn[:, 0:1]),
                                 jnp.sqrt(qn[1:2, :] * kn[:, DIFF_HD:DIFF_HD + 1])], axis=1) * BOUND_MARGIN
        fixed = jnp.max(bound - m) <= FIXED_OFFSET_SPAN
        sc_ref[...] = s
        ml_ref[0:1, :] = jnp.where(fixed, bound, m)

        def produce(c, s_ref, want_max):
            s = scores(k_ref[0, c * tk:(c + 1) * tk, :])
            s_ref[...] = s
            return jnp.max(s, axis=0, keepdims=True) if want_max else None

        def consume(s_ref, c, s_max, m, l):
            if s_max is None:
                p = jnp.exp2(s_ref[...] - m)
                l = l + jnp.sum(p, axis=0, keepdims=True)
                pv_prev = acc_ref[...]
                m_new = m
            else:
                m_new = jnp.maximum(m, s_max)
                alpha = jnp.exp2(m - m_new)
                p = jnp.exp2(s_ref[...] - m_new)
                l = alpha * l + jnp.sum(p, axis=0, keepdims=True)
                pv_prev = alpha * acc_ref[...]
            vt_t = vt_ref[0, :, c * tk:(c + 1) * tk]
            acc_ref[...] = pv_prev + jnp.dot(vt_t, p.astype(BF16), preferred_element_type=F32)
            return m_new, l

        def run_chunks(online):
            m = ml_ref[0:1, :]
            bufs = (sa_ref, sb_ref)
            s_max = produce(0, bufs[0], online)
            l = context_block(sc_ref[...], m)
            for c in range(n_chunks):
                if c + 1 < n_chunks:
                    next_max = produce(c + 1, bufs[(c + 1) % 2], online)
                m, l = consume(bufs[c % 2], c, s_max, m, l)
                s_max = next_max
            ml_ref[1:2, :] = l

        pl.when(fixed)(functools.partial(run_chunks, False))
        pl.when(jnp.logical_not(fixed))(functools.partial(run_chunks, True))
        l = ml_ref[1:2, :]

    lf = lam_ref[...]
    lam = (jnp.exp(jnp.sum(lf[0:1] * lf[1:2], axis=-1, keepdims=True))
           - jnp.exp(jnp.sum(lf[2:3] * lf[3:4], axis=-1, keepdims=True)) + lam_init)
    acc = acc_ref[...]
    inv = 1.0 / l
    o = acc[:, :tq] * inv[:, :tq] - lam * (acc[:, tq:] * inv[:, tq:])
    o = o * lax.rsqrt(jnp.mean(o * o, axis=0, keepdims=True) + NORM_EPS)
    o = (o * g_ref[...]) * (1.0 - lam_init)
    o_ref[0] = o.T.astype(BF16)


def _diff_attn(lam, subln_g, q, kc, vct, k, vt, lam_init):
    bq, sq, d = q.shape
    n_ctx = kc.shape[1]
    tq = 256
    hw = 2 * DIFF_HD
    in_specs = [_const_spec((4, DIFF_HD)), _const_spec((hw, 1)),
                pl.BlockSpec((1, tq, hw), lambda b, h, i: (b, i, h)),
                pl.BlockSpec((1, n_ctx, hw), lambda b, h, i: (b, 0, h)),
                pl.BlockSpec((1, hw, n_ctx), lambda b, h, i: (b, h, 0))]
    args = [lam, subln_g.reshape(hw, 1), q, kc, vct]
    scratch = [pltpu.VMEM((hw, 2 * tq), F32)]
    tk = 1024
    if k is not None:
        s = k.shape[1]
        assert s % tk == 0
        in_specs += [pl.BlockSpec((1, s, hw), lambda b, h, i: (b, 0, h)),
                     pl.BlockSpec((1, hw, s), lambda b, h, i: (b, h, 0))]
        args += [k, vt]
        scratch += [pltpu.VMEM((tk, 2 * tq), F32)] * 2 + [pltpu.VMEM((n_ctx, 2 * tq), F32),
                                                          pltpu.VMEM((2, 2 * tq), F32), pltpu.VMEM((1, hw), F32)]
    return pl.pallas_call(
        functools.partial(_diff_attn_kernel, lam_init=lam_init, tk=tk),
        out_shape=jax.ShapeDtypeStruct((bq, sq, d), BF16),
        grid=(bq, DIFF_HEADS, sq // tq),
        in_specs=in_specs,
        out_specs=pl.BlockSpec((1, tq, hw), lambda b, h, i: (b, i, h)),
        scratch_shapes=scratch,
        compiler_params=_cparams(3), name="diff_attn",
    )(*args)


def _nat_bias_rows():
    table = []
    for v, shift in enumerate((0, 4, 8)):
        rows = []
        for i in range(NAT_BAND):
            per_d = []
            for dq in range(NAT_QROWS):
                first = (0, dq, 4)[v]
                ok = first <= i < first + NAT_WIN_ROWS
                per_d.append(i - shift - dq + NAT_WIN_ROWS - 1 if ok else None)
            rows.append(per_d)
        table.append(rows)
    return table


def _nat_toeplitz_kernel(r_ref, o_ref):
    m, n = o_ref.shape

    def window(shape):
        col = lax.broadcasted_iota(jnp.int32, shape, 1)
        kc = lax.shift_right_logical(col, 7)
        qc = col & (GRID_W - 1)
        cs = jnp.clip(qc - NAT_WIN_COLS // 2, 0, GRID_W - NAT_WIN_COLS)
        return jnp.where(kc >= cs, jnp.where(kc < cs + NAT_WIN_COLS, kc - qc + NAT_WIN_COLS - 1, -1), -1)

    j = lax.broadcasted_iota(jnp.int32, (LANE, n), 0)
    onehot = jnp.where(j == window((LANE, n)), 1.0, 0.0).astype(BF16)
    r = r_ref[...]
    r_hi = r.astype(BF16)
    r1 = r - r_hi.astype(F32)
    r_mid = r1.astype(BF16)
    r_lo = (r1 - r_mid.astype(F32)).astype(BF16)
    t = (jnp.dot(r_hi, onehot, preferred_element_type=F32) + jnp.dot(r_mid, onehot, preferred_element_type=F32)
         + jnp.dot(r_lo, onehot, preferred_element_type=F32))
    o_ref[...] = jnp.where(window((m, n)) >= 0, t * LOG2_E, NEG)


def _nat_bias_kernel(t_ref, o_ref):
    lane = lax.broadcasted_iota(jnp.int32, (GRID_W, LANE), 1)
    neg = jnp.full((GRID_W, LANE), NEG, F32)
    rows = _nat_bias_rows()
    for v in range(3):
        for i in range(NAT_BAND):
            for pair in range(NAT_QROWS // 2):
                a_l, a_r = rows[v][i][2 * pair], rows[v][i][2 * pair + 1]
                left = neg if a_l is None else t_ref[0, a_l]
                right = neg if a_r is None else t_ref[0, a_r]
                o_ref[0, v, i * GRID_W:(i + 1) * GRID_W, pair * LANE:(pair + 1) * LANE] = (
                    jnp.where(lane < GRID_W, left, right))


def _nat_bias(rpb):
    nh, nr, nc = rpb.shape
    r2 = jnp.pad(rpb, ((0, 0), (0, 16 - nr), (0, LANE - nc))).reshape(nh * 16, LANE)
    n = GRID_W * LANE
    t = pl.pallas_call(
        _nat_toeplitz_kernel,
        out_shape=jax.ShapeDtypeStruct((nh * 16, n), F32),
        grid=(1,),
        in_specs=[pl.BlockSpec((nh * 16, LANE), lambda i: (0, 0))],
        out_specs=pl.BlockSpec((nh * 16, n), lambda i: (0, 0)),
        compiler_params=_cparams(1), name="nat_toeplitz",
    )(r2)
    t = t.reshape(nh, 16, GRID_W, LANE)
    nk, nq = NAT_BAND * GRID_W, NAT_QROWS * GRID_W
    return pl.pallas_call(
        _nat_bias_kernel,
        out_shape=jax.ShapeDtypeStruct((nh, 3, nk, nq), F32),
        grid=(nh,),
        in_specs=[pl.BlockSpec((1, 16, GRID_W, LANE), lambda h: (h, 0, 0, 0))],
        out_specs=pl.BlockSpec((1, 3, nk, nq), lambda h: (h, 0, 0, 0)),
        compiler_params=_cparams(1), name="nat_bias",
    )(t)


def _nat_attn_kernel(q_ref, kc_ref, vct_ref, *rest, n_rows, n_blk, group):
    if len(rest) == 4:
        bias_ref, k_ref, vt_ref, o_ref = rest
    else:
        bias_ref = k_ref = vt_ref = None
        (o_ref,) = rest
    tq = NAT_QROWS * GRID_W
    nk = NAT_BAND * GRID_W
    dn = (((1,), (1,)), ((), ()))
    lane = lax.broadcasted_iota(jnp.int32, (tq, LANE), 1)
    zero = jnp.zeros((tq, LANE), BF16)
    for g in range(group):
        q = q_ref[0, g * tq:(g + 1) * tq, :]
        if k_ref is not None:
            blk = pl.program_id(2) * group + g
            start = jnp.clip(blk * NAT_QROWS - NAT_WIN_ROWS // 2, 0, n_rows - NAT_BAND)
            off = pl.multiple_of(start * GRID_W, NAT_QROWS * GRID_W)
            kb = k_ref[0, pl.ds(off, nk), :]
            variant = jnp.where(blk == 0, 0, jnp.where(blk == n_blk - 1, 2, 1))
        outs = []
        for hh in range(LANE // NAT_HD):
            qh = jnp.where((lane >= hh * NAT_HD) & (lane < (hh + 1) * NAT_HD), q, zero)
            rows = slice(hh * NAT_HD, (hh + 1) * NAT_HD)
            s_c = lax.dot_general(kc_ref[0], qh, dn, preferred_element_type=F32)
            m = jnp.max(s_c, axis=0, keepdims=True)
            if k_ref is not None:
                s_b = lax.dot_general(kb, qh, dn, preferred_element_type=F32) + bias_ref[hh, variant]
                m = jnp.maximum(m, jnp.max(s_b, axis=0, keepdims=True))
            p_c = jnp.exp2(s_c - m)
            l = jnp.sum(p_c, axis=0, keepdims=True)
            o = jnp.dot(vct_ref[0, rows, :], p_c.astype(BF16), preferred_element_type=F32)
            if k_ref is not None:
                p_b = jnp.exp2(s_b - m)
                l = l + jnp.sum(p_b, axis=0, keepdims=True)
                o = o + jnp.dot(vt_ref[0, rows, pl.ds(off, nk)], p_b.astype(BF16), preferred_element_type=F32)
            outs.append(o / l)
        o_ref[0, g * tq:(g + 1) * tq, :] = jnp.concatenate(outs, axis=0).T.astype(BF16)


def _nat_attn(q, kc, vct, bias=None, k=None, vt=None):
    bq, sq, d = q.shape
    n_ctx = kc.shape[1]
    tq = NAT_QROWS * GRID_W
    n_blk = sq // tq
    group = math.gcd(n_blk, 4)
    in_specs = [pl.BlockSpec((1, group * tq, LANE), lambda hp, b, i: (b, i, hp)),
                pl.BlockSpec((1, n_ctx, LANE), lambda hp, b, i: (b, 0, hp)),
                pl.BlockSpec((1, LANE, n_ctx), lambda hp, b, i: (b, hp, 0))]
    args = [q, kc, vct]
    n_rows = None
    if k is not None:
        s = k.shape[1]
        n_rows = s // GRID_W
        nk = NAT_BAND * GRID_W
        in_specs += [pl.BlockSpec((LANE // NAT_HD, 3, nk, tq), lambda hp, b, i: (hp, 0, 0, 0)),
                     pl.BlockSpec((1, s, LANE), lambda hp, b, i: (b, 0, hp)),
                     pl.BlockSpec((1, LANE, s), lambda hp, b, i: (b, hp, 0))]
        args += [bias, k, vt]
    return pl.pallas_call(
        functools.partial(_nat_attn_kernel, n_rows=n_rows, n_blk=n_blk, group=group),
        out_shape=jax.ShapeDtypeStruct((bq, sq, d), BF16),
        grid=(d // LANE, bq, n_blk // group),
        in_specs=in_specs,
        out_specs=pl.BlockSpec((1, group * tq, LANE), lambda hp, b, i: (b, i, hp)),
        compiler_params=_cparams(3), name="nat_attn",
    )(*args)


def kernel(x, c, ctx, c_ctx, ada_w, ada_b, norm_g, ffn_w_in, ffn_w_out, pool_w, pool_scale,
           diff_w_qkv, diff_lam, diff_subln_g, diff_w_o, nat_w_qkv, nat_b_qkv, nat_rpb,
           nat_w_o, nat_b_o, final_g):
    b, seq, d = x.shape
    mods = _ada_mods(c, c_ctx, ada_w, ada_b)
    xc = ctx
    for i in range(DEPTH):
        kind = i % N_MIXERS
        j = i // N_MIXERS
        last = i == DEPTH - 1
        update_ctx = not last
        ctx_needed = update_ctx or kind != 0
        mx = mods[i, :b].reshape(b, N_MOD, d)
        mc = mods[i, b:b + 1].reshape(1, N_MOD, d)
        w_in = ffn_w_in[i].astype(BF16)
        w_out = ffn_w_out[i].astype(BF16)

        x = _ffn(x, mx, norm_g[i, 0], w_in[0], w_out[0])
        if ctx_needed:
            xc = _ffn(xc, mc, norm_g[i, 0], w_in[0], w_out[0])

        if kind == 0:
            tail_x = tail_c = dict(pool=(norm_g[i, 1], pool_w[j].astype(BF16), pool_scale[j]))
        elif kind == 1:
            lam_init = 0.8 - 0.6 * math.exp(-0.3 * i)
            scale = DIFF_HD ** -0.5 * LOG2_E
            q_l, k_l, vt_l = _qkv(x, mx, norm_g[i, 1], diff_w_qkv[j], scale, rope_tabs=_rope_tables(seq))
            q_c, k_c, vt_c = _qkv(xc, mc, norm_g[i, 1], diff_w_qkv[j], scale)
            o_l = _diff_attn(diff_lam[j], diff_subln_g[j], q_l, k_c, vt_c, k_l, vt_l, lam_init)
            tail_x = dict(proj=(o_l, diff_w_o[j], None))
            if update_ctx:
                o_c = _diff_attn(diff_lam[j], diff_subln_g[j], q_c, k_c, vt_c, None, None, lam_init)
                tail_c = dict(proj=(o_c, diff_w_o[j], None))
        else:
            scale = NAT_HD ** -0.5 * LOG2_E
            q_l, k_l, vt_l = _qkv(x, mx, norm_g[i, 1], nat_w_qkv[j], scale, b_qkv=nat_b_qkv[j])
            q_c, k_c, vt_c = _qkv(xc, mc, norm_g[i, 1], nat_w_qkv[j], scale, b_qkv=nat_b_qkv[j])
            o_l = _nat_attn(q_l, k_c, vt_c, _nat_bias(nat_rpb[j]), k_l, vt_l)
            tail_x = dict(proj=(o_l, nat_w_o[j], nat_b_o[j]))
            if update_ctx:
                o_c = _nat_attn(q_c, k_c, vt_c)
                tail_c = dict(proj=(o_c, nat_w_o[j], nat_b_o[j]))

        x = _ffn(x, mx, norm_g[i, 2], w_in[1], w_out[1], final_g=final_g if last else None, **tail_x)
        if update_ctx:
            xc = _ffn(xc, mc, norm_g[i, 2], w_in[1], w_out[1], **tail_c)
    return x
```

```python
import functools
import math

import numpy as np
import jax
import jax.numpy as jnp
from jax import lax
from jax.experimental import pallas as pl
from jax.experimental.pallas import tpu as pltpu

D_MODEL = 1024
DEPTH = 4
GRID_W = 64
N_MIXERS = 3
D_FF = 2816
N_MOD = 9
POOL_WINDOWS = (2, 4, 8, 16)
POOL_GW = 256
POOL_HALO = 8
POOL_SUBTILE = 256
DIFF_TILES_PER_STEP = 4
DIFF_HEADS = 8
DIFF_HD = 64
NAT_HEADS = 16
NAT_HD = 64
NAT_WIN_ROWS = 8
NAT_WIN_COLS = 16
NAT_QROWS = 4
NAT_BAND = 12
ROPE_THETA = 10000.0
NORM_EPS = 1e-6
NEG = -1e30
LOG2_E = math.log2(math.e)
BOUND_MARGIN = 1.01
FIXED_OFFSET_SPAN = 64.0

LANE = 128
FF_CHUNK = 256
VMEM_LIMIT = 56 * 1024 * 1024

F32 = jnp.float32
BF16 = jnp.bfloat16


def _cparams(n_axes):
    return pltpu.CompilerParams(dimension_semantics=("arbitrary",) * n_axes, vmem_limit_bytes=VMEM_LIMIT)


def _const_spec(shape):
    nd = len(shape)
    return pl.BlockSpec(shape, lambda *_: (0,) * nd, pipeline_mode=pl.Buffered(1))


def _mod_spec(mod):
    if mod.shape[0] == 1:
        return pl.BlockSpec((1, N_MOD, D_MODEL), lambda b, i: (0, 0, 0))
    return pl.BlockSpec((1, N_MOD, D_MODEL), lambda b, i: (b, 0, 0))


def _sigmoid(x):
    return 1.0 / (1.0 + jnp.exp(-x))


def _rms(x):
    return x * lax.rsqrt(jnp.mean(x * x, axis=-1, keepdims=True) + NORM_EPS)


def _norm_mod(x, g, mod_ref, row):
    shift = mod_ref[0, row:row + 1, :]
    scale = mod_ref[0, row + 1:row + 2, :]
    return (_rms(x) * g) * (1.0 + scale) + shift


def _ada_kernel(c_ref, w_ref, b_ref, o_ref):
    c = c_ref[...]
    s = (c * _sigmoid(c)).astype(BF16)
    o_ref[0] = jnp.dot(s, w_ref[0].astype(BF16), preferred_element_type=F32) + b_ref[0]


def _ada_mods(c, c_ctx, ada_w, ada_b):
    d = D_MODEL
    cc = jnp.concatenate([c, c_ctx[None], jnp.zeros((3, d), F32)], axis=0)
    tn = 1024
    return pl.pallas_call(
        _ada_kernel,
        out_shape=jax.ShapeDtypeStruct((DEPTH, 8, N_MOD * d), F32),
        grid=(DEPTH, N_MOD * d // tn),
        in_specs=[pl.BlockSpec((8, d), lambda l, n: (0, 0)),
                  pl.BlockSpec((1, d, tn), lambda l, n: (l, 0, n)),
                  pl.BlockSpec((1, 1, tn), lambda l, n: (l, 0, n))],
        out_specs=pl.BlockSpec((1, 8, tn), lambda l, n: (l, 0, n)),
        compiler_params=_cparams(2), name="ada_mods",
    )(cc, ada_w, ada_b.reshape(DEPTH, 1, N_MOD * d))


def _swiglu_half_step(x, mod_ref, g_ref, win_ref, wout_ref, a_ref, row, r0=0, side_work=()):
    rows = slice(r0, r0 + x.shape[0])
    side_work = list(side_work)
    h = _norm_mod(x, g_ref[...], mod_ref, row).astype(BF16)
    for j in range(D_FF // FF_CHUNK):
        lo = j * FF_CHUNK
        g = jnp.dot(h, win_ref[0, 0, :, lo:lo + FF_CHUNK], preferred_element_type=F32)
        u = jnp.dot(h, win_ref[0, 0, :, D_FF + lo:D_FF + lo + FF_CHUNK], preferred_element_type=F32)
        a_ref[rows, lo:lo + FF_CHUNK] = ((g * _sigmoid(g)) * u).astype(BF16)
        if side_work and j % 2 == 1:
            side_work.pop(0)()
    for thunk in side_work:
        thunk()
    y = jnp.dot(a_ref[rows, :], wout_ref[0, 0], preferred_element_type=F32)
    return x + (0.5 * mod_ref[0, row + 2:row + 3, :]) * y


def _pool_fill(x_ref, xp_ref, xn_ref, mod_ref, g_ref, ext_ref):
    i = pl.program_id(1)
    last = pl.num_programs(1) - 1
    tm = x_ref.shape[1]
    g = g_ref[...]
    ext_ref[0:POOL_HALO, :] = jnp.where(i > 0, _norm_mod(xp_ref[0], g, mod_ref, 3), 0.0)
    ext_ref[POOL_HALO:POOL_HALO + tm, :] = _norm_mod(x_ref[0], g, mod_ref, 3)
    ext_ref[POOL_HALO + tm:, :] = jnp.where(i < last, _norm_mod(xn_ref[0], g, mod_ref, 3), 0.0)


def _pool_group(x_ref, mod_ref, pw_ref, ps_ref, ext_ref, seq, r0, n, gi):
    tm = x_ref.shape[1]
    win = POOL_WINDOWS[gi]
    t = pl.program_id(1) * tm + r0 + lax.broadcasted_iota(jnp.int32, (n, 1), 0)
    base = POOL_HALO + r0
    cols = slice(gi * POOL_GW, (gi + 1) * POOL_GW)
    lo_off = -(win // 2)
    acc = ext_ref[base + lo_off:base + lo_off + n, cols]
    for o in range(lo_off + 1, lo_off + win):
        acc = acc + ext_ref[base + o:base + o + n, cols]
    lo = jnp.clip(t + lo_off, 0, seq - 1)
    hi = jnp.clip(t + lo_off + win - 1, 0, seq - 1)
    cnt = (hi - lo + 1).astype(F32)
    diff = (acc / cnt - ext_ref[base:base + n, cols]).astype(BF16)
    y = jnp.dot(diff, pw_ref[gi], preferred_element_type=F32) * ps_ref[:, cols]
    return x_ref[0, r0:r0 + n, cols] + mod_ref[0, 5:6, cols] * y


def _ffn_kernel(*refs, mixer, seq, bias, final):
    refs = list(refs)
    x_ref = refs.pop(0)
    if mixer == "pool":
        xp_ref, xn_ref = refs.pop(0), refs.pop(0)
    elif mixer == "proj":
        attn_ref = refs.pop(0)
    mod_ref = refs.pop(0)
    if mixer == "pool":
        g1_ref, pw_ref, ps_ref = refs.pop(0), refs.pop(0), refs.pop(0)
    elif mixer == "proj":
        wo_ref = refs.pop(0)
        bo_ref = refs.pop(0) if bias else None
    g_ref, win_ref, wout_ref = refs.pop(0), refs.pop(0), refs.pop(0)
    fg_ref = refs.pop(0) if final else None
    o_ref, a_ref = refs.pop(0), refs.pop(0)
    tm = x_ref.shape[1]

    def finish(out):
        return _rms(out) * fg_ref[...] if final else out

    if mixer == "pool":
        ext_ref = refs.pop(0)
        n = min(tm, POOL_SUBTILE)
        n_groups = len(POOL_WINDOWS)
        _pool_fill(x_ref, xp_ref, xn_ref, mod_ref, g1_ref, ext_ref)
        group = functools.partial(_pool_group, x_ref, mod_ref, pw_ref, ps_ref, ext_ref, seq)
        cur = [group(0, n, gi) for gi in range(n_groups)]
        for r0 in range(0, tm, n):
            nxt, side = [], []
            if r0 + n < tm:
                side = [lambda gi=gi, r=r0 + n: nxt.append(group(r, n, gi)) for gi in range(n_groups)]
            out = _swiglu_half_step(jnp.concatenate(cur, axis=1), mod_ref, g_ref, win_ref, wout_ref, a_ref, 6, r0, side)
            o_ref[0, r0:r0 + n, :] = finish(out)
            cur = nxt
        return
    if mixer == "proj":
        y = jnp.dot(attn_ref[0], wo_ref[...], preferred_element_type=F32)
        if bias:
            y = y + bo_ref[...]
        x = x_ref[0] + mod_ref[0, 5:6, :] * y
        row = 6
    else:
        x = x_ref[0]
        row = 0
    o_ref[0] = finish(_swiglu_half_step(x, mod_ref, g_ref, win_ref, wout_ref, a_ref, row))


def _ffn(x, mod, g, w_in, w_out, widx, pool=None, proj=None, final_g=None):
    bx, sx, d = x.shape
    tm = min(512, sx)
    tok = pl.BlockSpec((1, tm, d), lambda b, i: (b, i, 0))
    in_specs, args, scratch = [tok], [x], [pltpu.VMEM((tm, D_FF), BF16)]
    mixer = None
    if pool is not None:
        mixer = "pool"
        nb, n8 = tm // POOL_HALO, sx // POOL_HALO
        in_specs += [pl.BlockSpec((1, POOL_HALO, d), lambda b, i: (b, jnp.maximum(i * nb - 1, 0), 0)),
                     pl.BlockSpec((1, POOL_HALO, d), lambda b, i: (b, jnp.minimum((i + 1) * nb, n8 - 1), 0))]
        args += [x, x]
        scratch.append(pltpu.VMEM((tm + 2 * POOL_HALO, d), F32))
    elif proj is not None:
        mixer = "proj"
        in_specs.append(tok)
        args.append(proj[0])
    in_specs.append(_mod_spec(mod))
    args.append(mod)
    if pool is not None:
        in_specs += [_const_spec((1, d)), _const_spec((len(POOL_WINDOWS), POOL_GW, POOL_GW)), _const_spec((1, d))]
        args += [pool[0].reshape(1, d), pool[1], pool[2].reshape(1, d)]
    elif proj is not None:
        in_specs.append(_const_spec((d, d)))
        args.append(proj[1].astype(BF16))
        if proj[2] is not None:
            in_specs.append(_const_spec((1, d)))
            args.append(proj[2].reshape(1, d))
    in_specs += [_const_spec((1, d)),
                 pl.BlockSpec((1, 1, d, 2 * D_FF), lambda b, i: widx + (0, 0), pipeline_mode=pl.Buffered(1)),
                 pl.BlockSpec((1, 1, D_FF, d), lambda b, i: widx + (0, 0), pipeline_mode=pl.Buffered(1))]
    args += [g.reshape(1, d), w_in, w_out]
    if final_g is not None:
        in_specs.append(_const_spec((1, d)))
        args.append(final_g.reshape(1, d))
    return pl.pallas_call(
        functools.partial(_ffn_kernel, mixer=mixer, seq=sx, bias=proj is not None and proj[2] is not None,
                          final=final_g is not None),
        out_shape=jax.ShapeDtypeStruct(x.shape, F32),
        grid=(bx, sx // tm),
        in_specs=in_specs,
        out_specs=tok,
        scratch_shapes=scratch,
        compiler_params=_cparams(2), name="ffn" if mixer is None else mixer + "_ffn",
    )(*args)


def _rope_table_kernel(inv_ref, cos_ref, sin_ref):
    tm = cos_ref.shape[0]
    t = pl.program_id(0) * tm + lax.broadcasted_iota(jnp.int32, (tm, LANE), 0)
    lane = lax.broadcasted_iota(jnp.int32, (tm, LANE), 1)
    pos = jnp.where((lane & 63) < 32, lax.shift_right_logical(t, 6), t & (GRID_W - 1)).astype(F32)
    ang = pos * inv_ref[...]
    cos_ref[...] = jnp.cos(ang)
    sin_ref[...] = jnp.where((lane & 31) < 16, -jnp.sin(ang), jnp.sin(ang))


def _rope_tables(seq):
    per_axis = DIFF_HD // 2
    inv = ROPE_THETA ** (-jnp.arange(0, per_axis, 2, dtype=F32) / per_axis)
    inv_lane = jnp.tile(inv, LANE // inv.shape[0]).reshape(1, LANE)
    tm = 1024
    return pl.pallas_call(
        _rope_table_kernel,
        out_shape=(jax.ShapeDtypeStruct((seq, LANE), F32),) * 2,
        grid=(seq // tm,),
        in_specs=[pl.BlockSpec((1, LANE), lambda i: (0, 0))],
        out_specs=(pl.BlockSpec((tm, LANE), lambda i: (i, 0)),) * 2,
        compiler_params=_cparams(1), name="rope_tables",
    )(inv_lane)


def _qkv_kernel(x_ref, mod_ref, g_ref, wqk_ref, wvt_ref, *rest, rope, bias, scale):
    rest = list(rest)
    if bias:
        bqk_ref, bvt_ref = rest[:2]
        rest = rest[2:]
    if rope:
        cos_ref, sin_ref = rest[:2]
        rest = rest[2:]
    q_ref, k_ref, vt_ref = rest
    d = x_ref.shape[2]
    h = _norm_mod(x_ref[0], g_ref[...], mod_ref, 3).astype(BF16)
    vt = lax.dot_general(wvt_ref[...], h, (((1,), (1,)), ((), ())), preferred_element_type=F32)
    if bias:
        vt = vt + bvt_ref[...]
    vt_ref[0] = vt.astype(BF16)
    if rope:
        cos = cos_ref[...]
        sin = sin_ref[...]
        lane = lax.broadcasted_iota(jnp.int32, cos.shape, 1)
        first_half = (lane & 31) < 16
    for cb in range(2 * d // FF_CHUNK):
        c0 = cb * FF_CHUNK
        t2 = jnp.dot(h, wqk_ref[:, c0:c0 + FF_CHUNK], preferred_element_type=F32)
        if bias:
            t2 = t2 + bqk_ref[:, c0:c0 + FF_CHUNK]
        for half in range(FF_CHUNK // LANE):
            t = t2[:, half * LANE:(half + 1) * LANE]
            if rope:
                partner = jnp.where(first_half, pltpu.roll(t, LANE - 16, 1), pltpu.roll(t, 16, 1))
                t = t * cos + partner * sin
            col = c0 + half * LANE
            if col < d:
                q_ref[0, :, col:col + LANE] = (t * scale).astype(BF16)
            else:
                k_ref[0, :, col - d:col - d + LANE] = t.astype(BF16)


def _qkv(x, mod, g, w_qkv, scale, b_qkv=None, rope_tabs=None):
    bx, sx, d = x.shape
    tm = min(512, sx)
    wqk = w_qkv[:, :2 * d].astype(BF16)
    wvt = w_qkv[:, 2 * d:].T.astype(BF16)
    in_specs = [pl.BlockSpec((1, tm, d), lambda b, i: (b, i, 0)), _mod_spec(mod), _const_spec((1, d)),
                _const_spec((d, 2 * d)), _const_spec((d, d))]
    args = [x, mod, g.reshape(1, d), wqk, wvt]
    if b_qkv is not None:
        in_specs += [_const_spec((1, 2 * d)), _const_spec((d, 1))]
        args += [b_qkv[:2 * d].reshape(1, 2 * d), b_qkv[2 * d:].reshape(d, 1)]
    if rope_tabs is not None:
        in_specs += [pl.BlockSpec((tm, LANE), lambda b, i: (i, 0))] * 2
        args += list(rope_tabs)
    tok = pl.BlockSpec((1, tm, d), lambda b, i: (b, i, 0))
    return pl.pallas_call(
        functools.partial(_qkv_kernel, rope=rope_tabs is not None, bias=b_qkv is not None, scale=scale),
        out_shape=(jax.ShapeDtypeStruct((bx, sx, d), BF16), jax.ShapeDtypeStruct((bx, sx, d), BF16),
                   jax.ShapeDtypeStruct((bx, d, sx), BF16)),
        grid=(bx, sx // tm),
        in_specs=in_specs,
        out_specs=(tok, tok, pl.BlockSpec((1, d, tm), lambda b, i: (b, 0, i))),
        compiler_params=_cparams(2), name="qkv_proj",
    )(*args)


def _subhead_sq_norms(x, ones):
    xf = x.astype(F32)
    return jnp.dot((xf * xf).astype(BF16), ones, preferred_element_type=F32)


def _diff_attn_kernel(lam_ref, g_ref, q_ref, kc_ref, vct_ref, *rest, lam_init, tk, tq):
    if len(rest) == 9:
        k_ref, vt_ref, o_ref, acc_ref, sa_ref, sb_ref, sc_ref, ml_ref, kn_ref = rest
    else:
        k_ref = vt_ref = None
        o_ref, acc_ref = rest
        sa_ref = sb_ref = sc_ref = ml_ref = kn_ref = None
    n_sub = q_ref.shape[1] // tq
    hw = 2 * DIFF_HD
    n_chunks = None

    if k_ref is not None:
        n_chunks = k_ref.shape[1] // tk

        @pl.when(pl.program_id(2) == 0)
        def _():
            r = lax.broadcasted_iota(jnp.int32, (hw, hw), 0)
            c_ = lax.broadcasted_iota(jnp.int32, (hw, hw), 1)
            same_subhead = jnp.where((r < DIFF_HD) == (c_ < DIFF_HD), 1.0, 0.0).astype(BF16)
            kn = jnp.max(_subhead_sq_norms(kc_ref[0], same_subhead), axis=0, keepdims=True)
            for c in range(n_chunks):
                kn = jnp.maximum(kn, jnp.max(_subhead_sq_norms(k_ref[0, c * tk:(c + 1) * tk, :], same_subhead),
                                             axis=0, keepdims=True))
            kn_ref[...] = kn

    tile = functools.partial(
        _diff_attn_tile, lam_ref=lam_ref, g_ref=g_ref, q_ref=q_ref, kc_ref=kc_ref, vct_ref=vct_ref, k_ref=k_ref,
        vt_ref=vt_ref, o_ref=o_ref, acc_ref=acc_ref, sa_ref=sa_ref, sb_ref=sb_ref, sc_ref=sc_ref, ml_ref=ml_ref,
        kn_ref=kn_ref, lam_init=lam_init, tk=tk, tq=tq, hw=hw, n_chunks=n_chunks)
    if n_sub == 1:
        tile(slice(0, tq))
    else:
        def body(t, carry):
            tile(pl.ds(pl.multiple_of(t * tq, tq), tq))
            return carry

        lax.fori_loop(0, n_sub, body, 0)


def _diff_attn_tile(rows, *, lam_ref, g_ref, q_ref, kc_ref, vct_ref, k_ref, vt_ref, o_ref, acc_ref,
                    sa_ref, sb_ref, sc_ref, ml_ref, kn_ref, lam_init, tk, tq, hw, n_chunks):
    q = q_ref[0, rows, :]
    lane = lax.broadcasted_iota(jnp.int32, q.shape, 1)
    zero = jnp.zeros_like(q)
    qq = jnp.concatenate([jnp.where(lane < DIFF_HD, q, zero), jnp.where(lane >= DIFF_HD, q, zero)], axis=0)

    def scores(k_t):
        return lax.dot_general(k_t, qq, (((1,), (1,)), ((), ())), preferred_element_type=F32)

    s = scores(kc_ref[0])
    m = jnp.max(s, axis=0, keepdims=True)

    def context_block(s, m):
        p = jnp.exp2(s - m)
        acc_ref[...] = jnp.dot(vct_ref[0], p.astype(BF16), preferred_element_type=F32)
        return jnp.sum(p, axis=0, keepdims=True)

    if k_ref is None:
        l = context_block(s, m)
    else:
        r = lax.broadcasted_iota(jnp.int32, (16, hw), 0)
        c_ = lax.broadcasted_iota(jnp.int32, (16, hw), 1)
        select = jnp.where(r == jnp.where(c_ < DIFF_HD, 0, 1), 1.0, 0.0).astype(BF16)
        qf = q.astype(F32)
        qn = lax.dot_general(select, (qf * qf).astype(BF16), (((1,), (1,)), ((), ())),
                             preferred_element_type=F32)
        kn = kn_ref[...]
        bound = jnp.concatenate([jnp.sqrt(qn[0:1, :] * kn[:, 0:1]),
                                 jnp.sqrt(qn[1:2, :] * kn[:, DIFF_HD:DIFF_HD + 1])], axis=1) * BOUND_MARGIN
        fixed = jnp.max(bound - m) <= FIXED_OFFSET_SPAN
        sc_ref[...] = s
        ml_ref[0:1, :] = jnp.where(fixed, bound, m)

        def produce(c, s_ref, want_max):
            s = scores(k_ref[0, c * tk:(c + 1) * tk, :])
            s_ref[...] = s
            return jnp.max(s, axis=0, keepdims=True) if want_max else None

        def consume(s_ref, c, s_max, m, l):
            if s_max is None:
                p = jnp.exp2(s_ref[...] - m)
                l = l + jnp.sum(p, axis=0, keepdims=True)
                pv_prev = acc_ref[...]
                m_new = m
            else:
                m_new = jnp.maximum(m, s_max)
                alpha = jnp.exp2(m - m_new)
                p = jnp.exp2(s_ref[...] - m_new)
                l = alpha * l + jnp.sum(p, axis=0, keepdims=True)
                pv_prev = alpha * acc_ref[...]
            vt_t = vt_ref[0, :, c * tk:(c + 1) * tk]
            acc_ref[...] = pv_prev + jnp.dot(vt_t, p.astype(BF16), preferred_element_type=F32)
            return m_new, l

        def run_chunks(online):
            m = ml_ref[0:1, :]
            bufs = (sa_ref, sb_ref)
            s_max = produce(0, bufs[0], online)
            l = context_block(sc_ref[...], m)
            for c in range(n_chunks):
                if c + 1 < n_chunks:
                    next_max = produce(c + 1, bufs[(c + 1) % 2], online)
                m, l = consume(bufs[c % 2], c, s_max, m, l)
                s_max = next_max
            ml_ref[1:2, :] = l

        pl.when(fixed)(functools.partial(run_chunks, False))
        pl.when(jnp.logical_not(fixed))(functools.partial(run_chunks, True))
        l = ml_ref[1:2, :]

    lf = lam_ref[...]
    lam = (jnp.exp(jnp.sum(lf[0:1] * lf[1:2], axis=-1, keepdims=True))
           - jnp.exp(jnp.sum(lf[2:3] * lf[3:4], axis=-1, keepdims=True)) + lam_init)
    acc = acc_ref[...]
    inv = 1.0 / l
    o = acc[:, :tq] * inv[:, :tq] - lam * (acc[:, tq:] * inv[:, tq:])
    o = o * lax.rsqrt(jnp.mean(o * o, axis=0, keepdims=True) + NORM_EPS)
    o = (o * g_ref[...]) * (1.0 - lam_init)
    o_ref[0, rows, :] = o.T.astype(BF16)


def _diff_attn(lam, subln_g, q, kc, vct, k, vt, lam_init):
    bq, sq, d = q.shape
    n_ctx = kc.shape[1]
    tq = 256
    tq_step = tq * math.gcd(sq // tq, DIFF_TILES_PER_STEP)
    hw = 2 * DIFF_HD
    in_specs = [_const_spec((4, DIFF_HD)), _const_spec((hw, 1)),
                pl.BlockSpec((1, tq_step, hw), lambda b, h, i: (b, i, h)),
                pl.BlockSpec((1, n_ctx, hw), lambda b, h, i: (b, 0, h)),
                pl.BlockSpec((1, hw, n_ctx), lambda b, h, i: (b, h, 0))]
    args = [lam, subln_g.reshape(hw, 1), q, kc, vct]
    scratch = [pltpu.VMEM((hw, 2 * tq), F32)]
    tk = 1024
    if k is not None:
        s = k.shape[1]
        assert s % tk == 0
        in_specs += [pl.BlockSpec((1, s, hw), lambda b, h, i: (b, 0, h)),
                     pl.BlockSpec((1, hw, s), lambda b, h, i: (b, h, 0))]
        args += [k, vt]
        scratch += [pltpu.VMEM((tk, 2 * tq), F32)] * 2 + [pltpu.VMEM((n_ctx, 2 * tq), F32),
                                                          pltpu.VMEM((2, 2 * tq), F32), pltpu.VMEM((1, hw), F32)]
    return pl.pallas_call(
        functools.partial(_diff_attn_kernel, lam_init=lam_init, tk=tk, tq=tq),
        out_shape=jax.ShapeDtypeStruct((bq, sq, d), BF16),
        grid=(bq, DIFF_HEADS, sq // tq_step),
        in_specs=in_specs,
        out_specs=pl.BlockSpec((1, tq_step, hw), lambda b, h, i: (b, i, h)),
        scratch_shapes=scratch,
        compiler_params=_cparams(3), name="diff_attn",
    )(*args)


def _nat_bias_rows():
    table = []
    for v, shift in enumerate((0, 4, 8)):
        rows = []
        for i in range(NAT_BAND):
            per_d = []
            for dq in range(NAT_QROWS):
                first = (0, dq, 4)[v]
                ok = first <= i < first + NAT_WIN_ROWS
                per_d.append(i - shift - dq + NAT_WIN_ROWS - 1 if ok else None)
            rows.append(per_d)
        table.append(rows)
    return table


def _nat_toeplitz_kernel(r_ref, o_ref):
    m, n = o_ref.shape

    def window(shape):
        col = lax.broadcasted_iota(jnp.int32, shape, 1)
        kc = lax.shift_right_logical(col, 7)
        qc = col & (GRID_W - 1)
        cs = jnp.clip(qc - NAT_WIN_COLS // 2, 0, GRID_W - NAT_WIN_COLS)
        return jnp.where(kc >= cs, jnp.where(kc < cs + NAT_WIN_COLS, kc - qc + NAT_WIN_COLS - 1, -1), -1)

    j = lax.broadcasted_iota(jnp.int32, (LANE, n), 0)
    onehot = jnp.where(j == window((LANE, n)), 1.0, 0.0).astype(BF16)
    r = r_ref[...]
    r_hi = r.astype(BF16)
    r1 = r - r_hi.astype(F32)
    r_mid = r1.astype(BF16)
    r_lo = (r1 - r_mid.astype(F32)).astype(BF16)
    t = (jnp.dot(r_hi, onehot, preferred_element_type=F32) + jnp.dot(r_mid, onehot, preferred_element_type=F32)
         + jnp.dot(r_lo, onehot, preferred_element_type=F32))
    o_ref[...] = jnp.where(window((m, n)) >= 0, t * LOG2_E, NEG)


def _nat_bias_kernel(t_ref, o_ref):
    lane = lax.broadcasted_iota(jnp.int32, (GRID_W, LANE), 1)
    neg = jnp.full((GRID_W, LANE), NEG, F32)
    rows = _nat_bias_rows()
    for v in range(3):
        for i in range(NAT_BAND):
            for pair in range(NAT_QROWS // 2):
                a_l, a_r = rows[v][i][2 * pair], rows[v][i][2 * pair + 1]
                left = neg if a_l is None else t_ref[0, a_l]
                right = neg if a_r is None else t_ref[0, a_r]
                o_ref[0, v, i * GRID_W:(i + 1) * GRID_W, pair * LANE:(pair + 1) * LANE] = (
                    jnp.where(lane < GRID_W, left, right))


def _nat_bias(rpb):
    nh, nr, nc = rpb.shape
    r2 = jnp.pad(rpb, ((0, 0), (0, 16 - nr), (0, LANE - nc))).reshape(nh * 16, LANE)
    n = GRID_W * LANE
    t = pl.pallas_call(
        _nat_toeplitz_kernel,
        out_shape=jax.ShapeDtypeStruct((nh * 16, n), F32),
        grid=(1,),
        in_specs=[pl.BlockSpec((nh * 16, LANE), lambda i: (0, 0))],
        out_specs=pl.BlockSpec((nh * 16, n), lambda i: (0, 0)),
        compiler_params=_cparams(1), name="nat_toeplitz",
    )(r2)
    t = t.reshape(nh, 16, GRID_W, LANE)
    nk, nq = NAT_BAND * GRID_W, NAT_QROWS * GRID_W
    return pl.pallas_call(
        _nat_bias_kernel,
        out_shape=jax.ShapeDtypeStruct((nh, 3, nk, nq), F32),
        grid=(nh,),
        in_specs=[pl.BlockSpec((1, 16, GRID_W, LANE), lambda h: (h, 0, 0, 0))],
        out_specs=pl.BlockSpec((1, 3, nk, nq), lambda h: (h, 0, 0, 0)),
        compiler_params=_cparams(1), name="nat_bias",
    )(t)


def _nat_attn_kernel(q_ref, kc_ref, vct_ref, *rest, n_rows, n_blk, group):
    if len(rest) == 4:
        bias_ref, k_ref, vt_ref, o_ref = rest
    else:
        bias_ref = k_ref = vt_ref = None
        (o_ref,) = rest
    tq = NAT_QROWS * GRID_W
    nk = NAT_BAND * GRID_W
    dn = (((1,), (1,)), ((), ()))
    lane = lax.broadcasted_iota(jnp.int32, (tq, LANE), 1)
    zero = jnp.zeros((tq, LANE), BF16)
    for g in range(group):
        q = q_ref[0, g * tq:(g + 1) * tq, :]
        if k_ref is not None:
            blk = pl.program_id(2) * group + g
            start = jnp.clip(blk * NAT_QROWS - NAT_WIN_ROWS // 2, 0, n_rows - NAT_BAND)
            off = pl.multiple_of(start * GRID_W, NAT_QROWS * GRID_W)
            kb = k_ref[0, pl.ds(off, nk), :]
            variant = jnp.where(blk == 0, 0, jnp.where(blk == n_blk - 1, 2, 1))
        outs = []
        for hh in range(LANE // NAT_HD):
            qh = jnp.where((lane >= hh * NAT_HD) & (lane < (hh + 1) * NAT_HD), q, zero)
            rows = slice(hh * NAT_HD, (hh + 1) * NAT_HD)
            s_c = lax.dot_general(kc_ref[0], qh, dn, preferred_element_type=F32)
            m = jnp.max(s_c, axis=0, keepdims=True)
            if k_ref is not None:
                s_b = lax.dot_general(kb, qh, dn, preferred_element_type=F32) + bias_ref[hh, variant]
                m = jnp.maximum(m, jnp.max(s_b, axis=0, keepdims=True))
            p_c = jnp.exp2(s_c - m)
            l = jnp.sum(p_c, axis=0, keepdims=True)
            o = jnp.dot(vct_ref[0, rows, :], p_c.astype(BF16), preferred_element_type=F32)
            if k_ref is not None:
                p_b = jnp.exp2(s_b - m)
                l = l + jnp.sum(p_b, axis=0, keepdims=True)
                o = o + jnp.dot(vt_ref[0, rows, pl.ds(off, nk)], p_b.astype(BF16), preferred_element_type=F32)
            outs.append(o / l)
        o_ref[0, g * tq:(g + 1) * tq, :] = jnp.concatenate(outs, axis=0).T.astype(BF16)


def _nat_attn(q, kc, vct, bias=None, k=None, vt=None):
    bq, sq, d = q.shape
    n_ctx = kc.shape[1]
    tq = NAT_QROWS * GRID_W
    n_blk = sq // tq
    group = math.gcd(n_blk, 4)
    in_specs = [pl.BlockSpec((1, group * tq, LANE), lambda hp, b, i: (b, i, hp)),
                pl.BlockSpec((1, n_ctx, LANE), lambda hp, b, i: (b, 0, hp)),
                pl.BlockSpec((1, LANE, n_ctx), lambda hp, b, i: (b, hp, 0))]
    args = [q, kc, vct]
    n_rows = None
    if k is not None:
        s = k.shape[1]
        n_rows = s // GRID_W
        nk = NAT_BAND * GRID_W
        in_specs += [pl.BlockSpec((LANE // NAT_HD, 3, nk, tq), lambda hp, b, i: (hp, 0, 0, 0)),
                     pl.BlockSpec((1, s, LANE), lambda hp, b, i: (b, 0, hp)),
                     pl.BlockSpec((1, LANE, s), lambda hp, b, i: (b, hp, 0))]
        args += [bias, k, vt]
    return pl.pallas_call(
        functools.partial(_nat_attn_kernel, n_rows=n_rows, n_blk=n_blk, group=group),
        out_shape=jax.ShapeDtypeStruct((bq, sq, d), BF16),
        grid=(d // LANE, bq, n_blk // group),
        in_specs=in_specs,
        out_specs=pl.BlockSpec((1, group * tq, LANE), lambda hp, b, i: (b, i, hp)),
        compiler_params=_cparams(3), name="nat_attn",
    )(*args)


def kernel(x, c, ctx, c_ctx, ada_w, ada_b, norm_g, ffn_w_in, ffn_w_out, pool_w, pool_scale,
           diff_w_qkv, diff_lam, diff_subln_g, diff_w_o, nat_w_qkv, nat_b_qkv, nat_rpb,
           nat_w_o, nat_b_o, final_g):
    b, seq, d = x.shape
    mods = _ada_mods(c, c_ctx, ada_w, ada_b)
    w_in = ffn_w_in.astype(BF16)
    w_out = ffn_w_out.astype(BF16)
    xc = ctx
    for i in range(DEPTH):
        kind = i % N_MIXERS
        j = i // N_MIXERS
        last = i == DEPTH - 1
        update_ctx = not last
        ctx_needed = update_ctx or kind != 0
        mx = mods[i, :b].reshape(b, N_MOD, d)
        mc = mods[i, b:b + 1].reshape(1, N_MOD, d)

        x = _ffn(x, mx, norm_g[i, 0], w_in, w_out, (i, 0))
        if ctx_needed:
            xc = _ffn(xc, mc, norm_g[i, 0], w_in, w_out, (i, 0))

        if kind == 0:
            tail_x = tail_c = dict(pool=(norm_g[i, 1], pool_w[j].astype(BF16), pool_scale[j]))
        elif kind == 1:
            lam_init = 0.8 - 0.6 * math.exp(-0.3 * i)
            scale = DIFF_HD ** -0.5 * LOG2_E
            q_l, k_l, vt_l = _qkv(x, mx, norm_g[i, 1], diff_w_qkv[j], scale, rope_tabs=_rope_tables(seq))
            q_c, k_c, vt_c = _qkv(xc, mc, norm_g[i, 1], diff_w_qkv[j], scale)
            o_l = _diff_attn(diff_lam[j], diff_subln_g[j], q_l, k_c, vt_c, k_l, vt_l, lam_init)
            tail_x = dict(proj=(o_l, diff_w_o[j], None))
            if update_ctx:
                o_c = _diff_attn(diff_lam[j], diff_subln_g[j], q_c, k_c, vt_c, None, None, lam_init)
                tail_c = dict(proj=(o_c, diff_w_o[j], None))
        else:
            scale = NAT_HD ** -0.5 * LOG2_E
            q_l, k_l, vt_l = _qkv(x, mx, norm_g[i, 1], nat_w_qkv[j], scale, b_qkv=nat_b_qkv[j])
            q_c, k_c, vt_c = _qkv(xc, mc, norm_g[i, 1], nat_w_qkv[j], scale, b_qkv=nat_b_qkv[j])
            o_l = _nat_attn(q_l, k_c, vt_c, _nat_bias(nat_rpb[j]), k_l, vt_l)
            tail_x = dict(proj=(o_l, nat_w_o[j], nat_b_o[j]))
            if update_ctx:
                o_c = _nat_attn(q_c, k_c, vt_c)
                tail_c = dict(proj=(o_c, nat_w_o[j], nat_b_o[j]))

        x = _ffn(x, mx, norm_g[i, 2], w_in, w_out, (i, 1), final_g=final_g if last else None, **tail_x)
        if update_ctx:
            xc = _ffn(xc, mc, norm_g[i, 2], w_in, w_out, (i, 1), **tail_c)
    return x
```

```python
import functools
import math

import numpy as np
import jax
import jax.numpy as jnp
from jax import lax
from jax.experimental import pallas as pl
from jax.experimental.pallas import tpu as pltpu

D_MODEL = 1024
DEPTH = 4
GRID_W = 64
N_MIXERS = 3
D_FF = 2816
N_MOD = 9
POOL_WINDOWS = (2, 4, 8, 16)
POOL_GW = 256
POOL_HALO = 8
POOL_SUBTILE = 256
DIFF_TILES_PER_STEP = 4
DIFF_HEADS = 8
DIFF_HD = 64
NAT_HEADS = 16
NAT_HD = 64
NAT_WIN_ROWS = 8
NAT_WIN_COLS = 16
NAT_QROWS = 4
NAT_BAND = 12
ROPE_THETA = 10000.0
NORM_EPS = 1e-6
NEG = -1e30
LOG2_E = math.log2(math.e)
BOUND_MARGIN = 1.01
FIXED_OFFSET_SPAN = 64.0

LANE = 128
FF_CHUNK = 256
VMEM_LIMIT = 56 * 1024 * 1024

F32 = jnp.float32
BF16 = jnp.bfloat16


def _cparams(n_axes):
    return pltpu.CompilerParams(dimension_semantics=("arbitrary",) * n_axes, vmem_limit_bytes=VMEM_LIMIT)


def _const_spec(shape):
    nd = len(shape)
    return pl.BlockSpec(shape, lambda *_: (0,) * nd, pipeline_mode=pl.Buffered(1))


def _mod_spec(mod):
    if mod.shape[0] == 1:
        return pl.BlockSpec((1, N_MOD, D_MODEL), lambda b, i: (0, 0, 0))
    return pl.BlockSpec((1, N_MOD, D_MODEL), lambda b, i: (b, 0, 0))


def _sigmoid(x):
    return 1.0 / (1.0 + jnp.exp(-x))


def _rms(x):
    return x * lax.rsqrt(jnp.mean(x * x, axis=-1, keepdims=True) + NORM_EPS)


def _norm_mod(x, g, mod_ref, row):
    shift = mod_ref[0, row:row + 1, :]
    scale = mod_ref[0, row + 1:row + 2, :]
    return (_rms(x) * g) * (1.0 + scale) + shift


def _ada_kernel(c_ref, w_ref, b_ref, o_ref):
    c = c_ref[...]
    s = (c * _sigmoid(c)).astype(BF16)
    o_ref[0] = jnp.dot(s, w_ref[0].astype(BF16), preferred_element_type=F32) + b_ref[0]


def _ada_mods(c, c_ctx, ada_w, ada_b):
    d = D_MODEL
    cc = jnp.concatenate([c, c_ctx[None], jnp.zeros((3, d), F32)], axis=0)
    tn = 1024
    return pl.pallas_call(
        _ada_kernel,
        out_shape=jax.ShapeDtypeStruct((DEPTH, 8, N_MOD * d), F32),
        grid=(DEPTH, N_MOD * d // tn),
        in_specs=[pl.BlockSpec((8, d), lambda l, n: (0, 0)),
                  pl.BlockSpec((1, d, tn), lambda l, n: (l, 0, n)),
                  pl.BlockSpec((1, 1, tn), lambda l, n: (l, 0, n))],
        out_specs=pl.BlockSpec((1, 8, tn), lambda l, n: (l, 0, n)),
        compiler_params=_cparams(2), name="ada_mods",
    )(cc, ada_w, ada_b.reshape(DEPTH, 1, N_MOD * d))


def _swiglu_half_step(x, mod_ref, g_ref, win_ref, wout_ref, a_ref, row, r0=0, side_work=()):
    rows = slice(r0, r0 + x.shape[0])
    side_work = list(side_work)
    h = _norm_mod(x, g_ref[...], mod_ref, row).astype(BF16)
    for j in range(D_FF // FF_CHUNK):
        lo = j * FF_CHUNK
        g = jnp.dot(h, win_ref[0, 0, :, lo:lo + FF_CHUNK], preferred_element_type=F32)
        u = jnp.dot(h, win_ref[0, 0, :, D_FF + lo:D_FF + lo + FF_CHUNK], preferred_element_type=F32)
        a_ref[rows, lo:lo + FF_CHUNK] = ((g * _sigmoid(g)) * u).astype(BF16)
        if side_work and j % 2 == 1:
            side_work.pop(0)()
    for thunk in side_work:
        thunk()
    y = jnp.dot(a_ref[rows, :], wout_ref[0, 0], preferred_element_type=F32)
    return x + (0.5 * mod_ref[0, row + 2:row + 3, :]) * y


def _pool_fill(x_ref, xp_ref, xn_ref, mod_ref, g_ref, ext_ref):
    i = pl.program_id(1)
    last = pl.num_programs(1) - 1
    tm = x_ref.shape[1]
    g = g_ref[...]
    ext_ref[0:POOL_HALO, :] = jnp.where(i > 0, _norm_mod(xp_ref[0], g, mod_ref, 3), 0.0)
    ext_ref[POOL_HALO:POOL_HALO + tm, :] = _norm_mod(x_ref[0], g, mod_ref, 3)
    ext_ref[POOL_HALO + tm:, :] = jnp.where(i < last, _norm_mod(xn_ref[0], g, mod_ref, 3), 0.0)


def _pool_group(x_ref, mod_ref, pw_ref, ps_ref, ext_ref, seq, r0, n, gi):
    tm = x_ref.shape[1]
    win = POOL_WINDOWS[gi]
    t = pl.program_id(1) * tm + r0 + lax.broadcasted_iota(jnp.int32, (n, 1), 0)
    base = POOL_HALO + r0
    cols = slice(gi * POOL_GW, (gi + 1) * POOL_GW)
    lo_off = -(win // 2)
    acc = ext_ref[base + lo_off:base + lo_off + n, cols]
    for o in range(lo_off + 1, lo_off + win):
        acc = acc + ext_ref[base + o:base + o + n, cols]
    lo = jnp.clip(t + lo_off, 0, seq - 1)
    hi = jnp.clip(t + lo_off + win - 1, 0, seq - 1)
    cnt = (hi - lo + 1).astype(F32)
    diff = (acc / cnt - ext_ref[base:base + n, cols]).astype(BF16)
    y = jnp.dot(diff, pw_ref[gi], preferred_element_type=F32) * ps_ref[:, cols]
    return x_ref[0, r0:r0 + n, cols] + mod_ref[0, 5:6, cols] * y


def _ffn_kernel(*refs, mixer, seq, bias, final):
    refs = list(refs)
    x_ref = refs.pop(0)
    if mixer == "pool":
        xp_ref, xn_ref = refs.pop(0), refs.pop(0)
    elif mixer == "proj":
        attn_ref = refs.pop(0)
    mod_ref = refs.pop(0)
    if mixer == "pool":
        g1_ref, pw_ref, ps_ref = refs.pop(0), refs.pop(0), refs.pop(0)
    elif mixer == "proj":
        wo_ref = refs.pop(0)
        bo_ref = refs.pop(0) if bias else None
    g_ref, win_ref, wout_ref = refs.pop(0), refs.pop(0), refs.pop(0)
    fg_ref = refs.pop(0) if final else None
    o_ref, a_ref = refs.pop(0), refs.pop(0)
    tm = x_ref.shape[1]

    def finish(out):
        return _rms(out) * fg_ref[...] if final else out

    if mixer == "pool":
        ext_ref = refs.pop(0)
        n = min(tm, POOL_SUBTILE)
        n_groups = len(POOL_WINDOWS)
        _pool_fill(x_ref, xp_ref, xn_ref, mod_ref, g1_ref, ext_ref)
        group = functools.partial(_pool_group, x_ref, mod_ref, pw_ref, ps_ref, ext_ref, seq)
        cur = [group(0, n, gi) for gi in range(n_groups)]
        for r0 in range(0, tm, n):
            nxt, side = [], []
            if r0 + n < tm:
                side = [lambda gi=gi, r=r0 + n: nxt.append(group(r, n, gi)) for gi in range(n_groups)]
            out = _swiglu_half_step(jnp.concatenate(cur, axis=1), mod_ref, g_ref, win_ref, wout_ref, a_ref, 6, r0, side)
            o_ref[0, r0:r0 + n, :] = finish(out)
            cur = nxt
        return
    if mixer == "proj":
        y = jnp.dot(attn_ref[0], wo_ref[...], preferred_element_type=F32)
        if bias:
            y = y + bo_ref[...]
        x = x_ref[0] + mod_ref[0, 5:6, :] * y
        row = 6
    else:
        x = x_ref[0]
        row = 0
    o_ref[0] = finish(_swiglu_half_step(x, mod_ref, g_ref, win_ref, wout_ref, a_ref, row))


def _ffn(x, mod, g, w_in, w_out, widx, pool=None, proj=None, final_g=None):
    bx, sx, d = x.shape
    tm = min(512, sx)
    tok = pl.BlockSpec((1, tm, d), lambda b, i: (b, i, 0))
    in_specs, args, scratch = [tok], [x], [pltpu.VMEM((tm, D_FF), BF16)]
    mixer = None
    if pool is not None:
        mixer = "pool"
        nb, n8 = tm // POOL_HALO, sx // POOL_HALO
        in_specs += [pl.BlockSpec((1, POOL_HALO, d), lambda b, i: (b, jnp.maximum(i * nb - 1, 0), 0)),
                     pl.BlockSpec((1, POOL_HALO, d), lambda b, i: (b, jnp.minimum((i + 1) * nb, n8 - 1), 0))]
        args += [x, x]
        scratch.append(pltpu.VMEM((tm + 2 * POOL_HALO, d), F32))
    elif proj is not None:
        mixer = "proj"
        in_specs.append(tok)
        args.append(proj[0])
    in_specs.append(_mod_spec(mod))
    args.append(mod)
    if pool is not None:
        in_specs += [_const_spec((1, d)), _const_spec((len(POOL_WINDOWS), POOL_GW, POOL_GW)), _const_spec((1, d))]
        args += [pool[0].reshape(1, d), pool[1], pool[2].reshape(1, d)]
    elif proj is not None:
        in_specs.append(_const_spec((d, d)))
        args.append(proj[1].astype(BF16))
        if proj[2] is not None:
            in_specs.append(_const_spec((1, d)))
            args.append(proj[2].reshape(1, d))
    in_specs += [_const_spec((1, d)),
                 pl.BlockSpec((1, 1, d, 2 * D_FF), lambda b, i: widx + (0, 0), pipeline_mode=pl.Buffered(1)),
                 pl.BlockSpec((1, 1, D_FF, d), lambda b, i: widx + (0, 0), pipeline_mode=pl.Buffered(1))]
    args += [g.reshape(1, d), w_in, w_out]
    if final_g is not None:
        in_specs.append(_const_spec((1, d)))
        args.append(final_g.reshape(1, d))
    return pl.pallas_call(
        functools.partial(_ffn_kernel, mixer=mixer, seq=sx, bias=proj is not None and proj[2] is not None,
                          final=final_g is not None),
        out_shape=jax.ShapeDtypeStruct(x.shape, F32),
        grid=(bx, sx // tm),
        in_specs=in_specs,
        out_specs=tok,
        scratch_shapes=scratch,
        compiler_params=_cparams(2), name="ffn" if mixer is None else mixer + "_ffn",
    )(*args)


def _rope_table_kernel(inv_ref, cos_ref, sin_ref):
    tm = cos_ref.shape[0]
    t = pl.program_id(0) * tm + lax.broadcasted_iota(jnp.int32, (tm, LANE), 0)
    lane = lax.broadcasted_iota(jnp.int32, (tm, LANE), 1)
    pos = jnp.where((lane & 63) < 32, lax.shift_right_logical(t, 6), t & (GRID_W - 1)).astype(F32)
    ang = pos * inv_ref[...]
    cos_ref[...] = jnp.cos(ang)
    sin_ref[...] = jnp.where((lane & 31) < 16, -jnp.sin(ang), jnp.sin(ang))


def _rope_tables(seq):
    per_axis = DIFF_HD // 2
    inv = ROPE_THETA ** (-jnp.arange(0, per_axis, 2, dtype=F32) / per_axis)
    inv_lane = jnp.tile(inv, LANE // inv.shape[0]).reshape(1, LANE)
    tm = 1024
    return pl.pallas_call(
        _rope_table_kernel,
        out_shape=(jax.ShapeDtypeStruct((seq, LANE), F32),) * 2,
        grid=(seq // tm,),
        in_specs=[pl.BlockSpec((1, LANE), lambda i: (0, 0))],
        out_specs=(pl.BlockSpec((tm, LANE), lambda i: (i, 0)),) * 2,
        compiler_params=_cparams(1), name="rope_tables",
    )(inv_lane)


def _qkv_kernel(x_ref, mod_ref, g_ref, wqk_ref, wvt_ref, *rest, rope, bias, scale):
    rest = list(rest)
    if bias:
        bqk_ref, bvt_ref = rest[:2]
        rest = rest[2:]
    if rope:
        cos_ref, sin_ref = rest[:2]
        rest = rest[2:]
    q_ref, k_ref, vt_ref = rest
    d = x_ref.shape[2]
    h = _norm_mod(x_ref[0], g_ref[...], mod_ref, 3).astype(BF16)
    vt = lax.dot_general(wvt_ref[...], h, (((1,), (1,)), ((), ())), preferred_element_type=F32)
    if bias:
        vt = vt + bvt_ref[...]
    vt_ref[0] = vt.astype(BF16)
    if rope:
        cos = cos_ref[...]
        sin = sin_ref[...]
        lane = lax.broadcasted_iota(jnp.int32, cos.shape, 1)
        first_half = (lane & 31) < 16
    for cb in range(2 * d // FF_CHUNK):
        c0 = cb * FF_CHUNK
        t2 = jnp.dot(h, wqk_ref[:, c0:c0 + FF_CHUNK], preferred_element_type=F32)
        if bias:
            t2 = t2 + bqk_ref[:, c0:c0 + FF_CHUNK]
        for half in range(FF_CHUNK // LANE):
            t = t2[:, half * LANE:(half + 1) * LANE]
            if rope:
                partner = jnp.where(first_half, pltpu.roll(t, LANE - 16, 1), pltpu.roll(t, 16, 1))
                t = t * cos + partner * sin
            col = c0 + half * LANE
            if col < d:
                q_ref[0, :, col:col + LANE] = (t * scale).astype(BF16)
            else:
                k_ref[0, :, col - d:col - d + LANE] = t.astype(BF16)


def _qkv(x, mod, g, w_qkv, scale, b_qkv=None, rope_tabs=None):
    bx, sx, d = x.shape
    tm = min(512, sx)
    wqk = w_qkv[:, :2 * d].astype(BF16)
    wvt = w_qkv[:, 2 * d:].T.astype(BF16)
    in_specs = [pl.BlockSpec((1, tm, d), lambda b, i: (b, i, 0)), _mod_spec(mod), _const_spec((1, d)),
                _const_spec((d, 2 * d)), _const_spec((d, d))]
    args = [x, mod, g.reshape(1, d), wqk, wvt]
    if b_qkv is not None:
        in_specs += [_const_spec((1, 2 * d)), _const_spec((d, 1))]
        args += [b_qkv[:2 * d].reshape(1, 2 * d), b_qkv[2 * d:].reshape(d, 1)]
    if rope_tabs is not None:
        in_specs += [pl.BlockSpec((tm, LANE), lambda b, i: (i, 0))] * 2
        args += list(rope_tabs)
    tok = pl.BlockSpec((1, tm, d), lambda b, i: (b, i, 0))
    return pl.pallas_call(
        functools.partial(_qkv_kernel, rope=rope_tabs is not None, bias=b_qkv is not None, scale=scale),
        out_shape=(jax.ShapeDtypeStruct((bx, sx, d), BF16), jax.ShapeDtypeStruct((bx, sx, d), BF16),
                   jax.ShapeDtypeStruct((bx, d, sx), BF16)),
        grid=(bx, sx // tm),
        in_specs=in_specs,
        out_specs=(tok, tok, pl.BlockSpec((1, d, tm), lambda b, i: (b, 0, i))),
        compiler_params=_cparams(2), name="qkv_proj",
    )(*args)


def _subhead_sq_norms(x, ones):
    xf = x.astype(F32)
    return jnp.dot((xf * xf).astype(BF16), ones, preferred_element_type=F32)


def _diff_attn_kernel(lam_ref, g_ref, q_ref, kc_ref, vct_ref, *rest, lam_init, tk, tq):
    if len(rest) == 9:
        k_ref, vt_ref, o_ref, acc_ref, sa_ref, sb_ref, sc_ref, ml_ref, kn_ref = rest
    else:
        k_ref = vt_ref = None
        o_ref, acc_ref = rest
        sa_ref = sb_ref = sc_ref = ml_ref = kn_ref = None
    n_sub = q_ref.shape[1] // tq
    hw = 2 * DIFF_HD
    n_chunks = None

    if k_ref is not None:
        n_chunks = k_ref.shape[1] // tk

        @pl.when(pl.program_id(2) == 0)
        def _():
            r = lax.broadcasted_iota(jnp.int32, (hw, hw), 0)
            c_ = lax.broadcasted_iota(jnp.int32, (hw, hw), 1)
            same_subhead = jnp.where((r < DIFF_HD) == (c_ < DIFF_HD), 1.0, 0.0).astype(BF16)
            kn = jnp.max(_subhead_sq_norms(kc_ref[0], same_subhead), axis=0, keepdims=True)
            for c in range(n_chunks):
                kn = jnp.maximum(kn, jnp.max(_subhead_sq_norms(k_ref[0, c * tk:(c + 1) * tk, :], same_subhead),
                                             axis=0, keepdims=True))
            kn_ref[...] = kn

    def split_subheads(rows):
        q = q_ref[0, rows, :]
        lane = lax.broadcasted_iota(jnp.int32, q.shape, 1)
        zero = jnp.zeros_like(q)
        return q, jnp.concatenate([jnp.where(lane < DIFF_HD, q, zero), jnp.where(lane >= DIFF_HD, q, zero)], axis=0)

    def scores(k_t, qq):
        return lax.dot_general(k_t, qq, (((1,), (1,)), ((), ())), preferred_element_type=F32)

    def context_block(s, m):
        p = jnp.exp2(s - m)
        acc_ref[...] = jnp.dot(vct_ref[0], p.astype(BF16), preferred_element_type=F32)
        return jnp.sum(p, axis=0, keepdims=True)

    def finish(rows, l):
        lf = lam_ref[...]
        lam = (jnp.exp(jnp.sum(lf[0:1] * lf[1:2], axis=-1, keepdims=True))
               - jnp.exp(jnp.sum(lf[2:3] * lf[3:4], axis=-1, keepdims=True)) + lam_init)
        acc = acc_ref[...]
        inv = 1.0 / l
        o = acc[:, :tq] * inv[:, :tq] - lam * (acc[:, tq:] * inv[:, tq:])
        o = o * lax.rsqrt(jnp.mean(o * o, axis=0, keepdims=True) + NORM_EPS)
        o = (o * g_ref[...]) * (1.0 - lam_init)
        o_ref[0, rows, :] = o.T.astype(BF16)

    if k_ref is None:
        assert n_sub == 1
        rows = slice(0, tq)
        _, qq = split_subheads(rows)
        s = scores(kc_ref[0], qq)
        finish(rows, context_block(s, jnp.max(s, axis=0, keepdims=True)))
        return

    r = lax.broadcasted_iota(jnp.int32, (16, hw), 0)
    c_ = lax.broadcasted_iota(jnp.int32, (16, hw), 1)
    select = jnp.where(r == jnp.where(c_ < DIFF_HD, 0, 1), 1.0, 0.0).astype(BF16)
    kn = kn_ref[...]
    gap = None
    for t in range(n_sub):
        q, qq = split_subheads(slice(t * tq, (t + 1) * tq))
        s = scores(kc_ref[0], qq)
        m = jnp.max(s, axis=0, keepdims=True)
        qf = q.astype(F32)
        qn = lax.dot_general(select, (qf * qf).astype(BF16), (((1,), (1,)), ((), ())),
                             preferred_element_type=F32)
        bound = jnp.concatenate([jnp.sqrt(qn[0:1, :] * kn[:, 0:1]),
                                 jnp.sqrt(qn[1:2, :] * kn[:, DIFF_HD:DIFF_HD + 1])], axis=1) * BOUND_MARGIN
        sc_ref[t] = s
        ml_ref[t, 0:1, :] = m
        ml_ref[t, 1:2, :] = bound
        gap = bound - m if gap is None else jnp.maximum(gap, bound - m)
    fixed = jnp.max(gap) <= FIXED_OFFSET_SPAN

    def run_tiles(online):
        def produce(c, s_ref, qq):
            s = scores(k_ref[0, c * tk:(c + 1) * tk, :], qq)
            s_ref[...] = s
            return jnp.max(s, axis=0, keepdims=True) if online else None

        def consume(s_ref, c, s_max, m, l):
            if online:
                m_new = jnp.maximum(m, s_max)
                alpha = jnp.exp2(m - m_new)
                p = jnp.exp2(s_ref[...] - m_new)
                l = alpha * l + jnp.sum(p, axis=0, keepdims=True)
                pv_prev = alpha * acc_ref[...]
            else:
                p = jnp.exp2(s_ref[...] - m)
                l = l + jnp.sum(p, axis=0, keepdims=True)
                pv_prev = acc_ref[...]
                m_new = m
            vt_t = vt_ref[0, :, c * tk:(c + 1) * tk]
            acc_ref[...] = pv_prev + jnp.dot(vt_t, p.astype(BF16), preferred_element_type=F32)
            return m_new, l

        def tile(t, carry):
            rows = pl.ds(pl.multiple_of(t * tq, tq), tq)
            _, qq = split_subheads(rows)
            m = ml_ref[t, 0:1, :] if online else ml_ref[t, 1:2, :]
            bufs = (sa_ref, sb_ref)
            s_max = produce(0, bufs[0], qq)
            l = context_block(sc_ref[t], m)
            for c in range(n_chunks):
                if c + 1 < n_chunks:
                    next_max = produce(c + 1, bufs[(c + 1) % 2], qq)
                m, l = consume(bufs[c % 2], c, s_max, m, l)
                s_max = next_max
            finish(rows, l)
            return carry

        lax.fori_loop(0, n_sub, tile, 0)

    pl.when(fixed)(functools.partial(run_tiles, False))
    pl.when(jnp.logical_not(fixed))(functools.partial(run_tiles, True))


def _diff_attn(lam, subln_g, q, kc, vct, k, vt, lam_init):
    bq, sq, d = q.shape
    n_ctx = kc.shape[1]
    tq = 256
    tq_step = tq * math.gcd(sq // tq, DIFF_TILES_PER_STEP)
    hw = 2 * DIFF_HD
    in_specs = [_const_spec((4, DIFF_HD)), _const_spec((hw, 1)),
                pl.BlockSpec((1, tq_step, hw), lambda b, h, i: (b, i, h)),
                pl.BlockSpec((1, n_ctx, hw), lambda b, h, i: (b, 0, h)),
                pl.BlockSpec((1, hw, n_ctx), lambda b, h, i: (b, h, 0))]
    args = [lam, subln_g.reshape(hw, 1), q, kc, vct]
    scratch = [pltpu.VMEM((hw, 2 * tq), F32)]
    tk = 1024
    if k is not None:
        s = k.shape[1]
        assert s % tk == 0
        in_specs += [pl.BlockSpec((1, s, hw), lambda b, h, i: (b, 0, h)),
                     pl.BlockSpec((1, hw, s), lambda b, h, i: (b, h, 0))]
        args += [k, vt]
        n_sub = tq_step // tq
        scratch += [pltpu.VMEM((tk, 2 * tq), F32)] * 2 + [pltpu.VMEM((n_sub, n_ctx, 2 * tq), F32),
                                                          pltpu.VMEM((n_sub, 2, 2 * tq), F32), pltpu.VMEM((1, hw), F32)]
    return pl.pallas_call(
        functools.partial(_diff_attn_kernel, lam_init=lam_init, tk=tk, tq=tq),
        out_shape=jax.ShapeDtypeStruct((bq, sq, d), BF16),
        grid=(bq, DIFF_HEADS, sq // tq_step),
        in_specs=in_specs,
        out_specs=pl.BlockSpec((1, tq_step, hw), lambda b, h, i: (b, i, h)),
        scratch_shapes=scratch,
        compiler_params=_cparams(3), name="diff_attn",
    )(*args)


def _nat_bias_rows():
    table = []
    for v, shift in enumerate((0, 4, 8)):
        rows = []
        for i in range(NAT_BAND):
            per_d = []
            for dq in range(NAT_QROWS):
                first = (0, dq, 4)[v]
                ok = first <= i < first + NAT_WIN_ROWS
                per_d.append(i - shift - dq + NAT_WIN_ROWS - 1 if ok else None)
            rows.append(per_d)
        table.append(rows)
    return table


def _nat_toeplitz_kernel(r_ref, o_ref):
    m, n = o_ref.shape

    def window(shape):
        col = lax.broadcasted_iota(jnp.int32, shape, 1)
        kc = lax.shift_right_logical(col, 7)
        qc = col & (GRID_W - 1)
        cs = jnp.clip(qc - NAT_WIN_COLS // 2, 0, GRID_W - NAT_WIN_COLS)
        return jnp.where(kc >= cs, jnp.where(kc < cs + NAT_WIN_COLS, kc - qc + NAT_WIN_COLS - 1, -1), -1)

    j = lax.broadcasted_iota(jnp.int32, (LANE, n), 0)
    onehot = jnp.where(j == window((LANE, n)), 1.0, 0.0).astype(BF16)
    r = r_ref[...]
    r_hi = r.astype(BF16)
    r1 = r - r_hi.astype(F32)
    r_mid = r1.astype(BF16)
    r_lo = (r1 - r_mid.astype(F32)).astype(BF16)
    t = (jnp.dot(r_hi, onehot, preferred_element_type=F32) + jnp.dot(r_mid, onehot, preferred_element_type=F32)
         + jnp.dot(r_lo, onehot, preferred_element_type=F32))
    o_ref[...] = jnp.where(window((m, n)) >= 0, t * LOG2_E, NEG)


def _nat_bias_kernel(t_ref, o_ref):
    lane = lax.broadcasted_iota(jnp.int32, (GRID_W, LANE), 1)
    neg = jnp.full((GRID_W, LANE), NEG, F32)
    rows = _nat_bias_rows()
    for v in range(3):
        for i in range(NAT_BAND):
            for pair in range(NAT_QROWS // 2):
                a_l, a_r = rows[v][i][2 * pair], rows[v][i][2 * pair + 1]
                left = neg if a_l is None else t_ref[0, a_l]
                right = neg if a_r is None else t_ref[0, a_r]
                o_ref[0, v, i * GRID_W:(i + 1) * GRID_W, pair * LANE:(pair + 1) * LANE] = (
                    jnp.where(lane < GRID_W, left, right))


def _nat_bias(rpb):
    nh, nr, nc = rpb.shape
    r2 = jnp.pad(rpb, ((0, 0), (0, 16 - nr), (0, LANE - nc))).reshape(nh * 16, LANE)
    n = GRID_W * LANE
    t = pl.pallas_call(
        _nat_toeplitz_kernel,
        out_shape=jax.ShapeDtypeStruct((nh * 16, n), F32),
        grid=(1,),
        in_specs=[pl.BlockSpec((nh * 16, LANE), lambda i: (0, 0))],
        out_specs=pl.BlockSpec((nh * 16, n), lambda i: (0, 0)),
        compiler_params=_cparams(1), name="nat_toeplitz",
    )(r2)
    t = t.reshape(nh, 16, GRID_W, LANE)
    nk, nq = NAT_BAND * GRID_W, NAT_QROWS * GRID_W
    return pl.pallas_call(
        _nat_bias_kernel,
        out_shape=jax.ShapeDtypeStruct((nh, 3, nk, nq), F32),
        grid=(nh,),
        in_specs=[pl.BlockSpec((1, 16, GRID_W, LANE), lambda h: (h, 0, 0, 0))],
        out_specs=pl.BlockSpec((1, 3, nk, nq), lambda h: (h, 0, 0, 0)),
        compiler_params=_cparams(1), name="nat_bias",
    )(t)


def _nat_attn_kernel(q_ref, kc_ref, vct_ref, *rest, n_rows, n_blk, group):
    if len(rest) == 6:
        bias_ref, k_ref, vt_ref, o_ref, sa_ref, sb_ref = rest
    else:
        bias_ref = k_ref = vt_ref = None
        o_ref, sa_ref, sb_ref = rest
    tq = NAT_QROWS * GRID_W
    nk = NAT_BAND * GRID_W
    n_ctx = kc_ref.shape[1]
    n_heads = LANE // NAT_HD
    dn = (((1,), (1,)), ((), ()))
    lane = lax.broadcasted_iota(jnp.int32, (tq, LANE), 1)
    zero = jnp.zeros((tq, LANE), BF16)

    def band(g):
        blk = pl.program_id(2) * group + g
        start = jnp.clip(blk * NAT_QROWS - NAT_WIN_ROWS // 2, 0, n_rows - NAT_BAND)
        return (pl.multiple_of(start * GRID_W, NAT_QROWS * GRID_W),
                jnp.where(blk == 0, 0, jnp.where(blk == n_blk - 1, 2, 1)))

    def produce(g, hh, s_ref):
        q = q_ref[0, g * tq:(g + 1) * tq, :]
        qh = jnp.where((lane >= hh * NAT_HD) & (lane < (hh + 1) * NAT_HD), q, zero)
        s_c = lax.dot_general(kc_ref[0], qh, dn, preferred_element_type=F32)
        s_ref[0:n_ctx, :] = s_c
        m = jnp.max(s_c, axis=0, keepdims=True)
        if k_ref is not None:
            off, variant = band(g)
            s_b = lax.dot_general(k_ref[0, pl.ds(off, nk), :], qh, dn, preferred_element_type=F32)
            s_b = s_b + bias_ref[hh, variant]
            s_ref[n_ctx:, :] = s_b
            m = jnp.maximum(m, jnp.max(s_b, axis=0, keepdims=True))
        return m

    def consume(g, hh, s_ref, m):
        rows = slice(hh * NAT_HD, (hh + 1) * NAT_HD)
        p = jnp.exp2(s_ref[...] - m)
        l = jnp.sum(p, axis=0, keepdims=True)
        pb = p.astype(BF16)
        o = jnp.dot(vct_ref[0, rows, :], pb[0:n_ctx], preferred_element_type=F32)
        if k_ref is not None:
            off, _ = band(g)
            o = o + jnp.dot(vt_ref[0, rows, pl.ds(off, nk)], pb[n_ctx:], preferred_element_type=F32)
        return o / l

    chains = [(g, hh) for g in range(group) for hh in range(n_heads)]
    bufs = (sa_ref, sb_ref)
    m = produce(*chains[0], bufs[0])
    outs = []
    for i, (g, hh) in enumerate(chains):
        if i + 1 < len(chains):
            m_next = produce(*chains[i + 1], bufs[(i + 1) % 2])
        outs.append(consume(g, hh, bufs[i % 2], m))
        m = m_next
        if hh == n_heads - 1:
            o_ref[0, g * tq:(g + 1) * tq, :] = jnp.concatenate(outs, axis=0).T.astype(BF16)
            outs = []


def _nat_attn(q, kc, vct, bias=None, k=None, vt=None):
    bq, sq, d = q.shape
    n_ctx = kc.shape[1]
    tq = NAT_QROWS * GRID_W
    n_blk = sq // tq
    group = math.gcd(n_blk, 4)
    in_specs = [pl.BlockSpec((1, group * tq, LANE), lambda hp, b, i: (b, i, hp)),
                pl.BlockSpec((1, n_ctx, LANE), lambda hp, b, i: (b, 0, hp)),
                pl.BlockSpec((1, LANE, n_ctx), lambda hp, b, i: (b, hp, 0))]
    args = [q, kc, vct]
    n_rows = None
    n_keys = n_ctx
    if k is not None:
        s = k.shape[1]
        n_rows = s // GRID_W
        nk = NAT_BAND * GRID_W
        n_keys += nk
        in_specs += [pl.BlockSpec((LANE // NAT_HD, 3, nk, tq), lambda hp, b, i: (hp, 0, 0, 0)),
                     pl.BlockSpec((1, s, LANE), lambda hp, b, i: (b, 0, hp)),
                     pl.BlockSpec((1, LANE, s), lambda hp, b, i: (b, hp, 0))]
        args += [bias, k, vt]
    return pl.pallas_call(
        functools.partial(_nat_attn_kernel, n_rows=n_rows, n_blk=n_blk, group=group),
        out_shape=jax.ShapeDtypeStruct((bq, sq, d), BF16),
        grid=(d // LANE, bq, n_blk // group),
        in_specs=in_specs,
        out_specs=pl.BlockSpec((1, group * tq, LANE), lambda hp, b, i: (b, i, hp)),
        scratch_shapes=[pltpu.VMEM((n_keys, tq), F32)] * 2,
        compiler_params=_cparams(3), name="nat_attn",
    )(*args)


def kernel(x, c, ctx, c_ctx, ada_w, ada_b, norm_g, ffn_w_in, ffn_w_out, pool_w, pool_scale,
           diff_w_qkv, diff_lam, diff_subln_g, diff_w_o, nat_w_qkv, nat_b_qkv, nat_rpb,
           nat_w_o, nat_b_o, final_g):
    b, seq, d = x.shape
    mods = _ada_mods(c, c_ctx, ada_w, ada_b)
    w_in = ffn_w_in.astype(BF16)
    w_out = ffn_w_out.astype(BF16)
    xc = ctx
    for i in range(DEPTH):
        kind = i % N_MIXERS
        j = i // N_MIXERS
        last = i == DEPTH - 1
        update_ctx = not last
        ctx_needed = update_ctx or kind != 0
        mx = mods[i, :b].reshape(b, N_MOD, d)
        mc = mods[i, b:b + 1].reshape(1, N_MOD, d)

        x = _ffn(x, mx, norm_g[i, 0], w_in, w_out, (i, 0))
        if ctx_needed:
            xc = _ffn(xc, mc, norm_g[i, 0], w_in, w_out, (i, 0))

        if kind == 0:
            tail_x = tail_c = dict(pool=(norm_g[i, 1], pool_w[j].astype(BF16), pool_scale[j]))
        elif kind == 1:
            lam_init = 0.8 - 0.6 * math.exp(-0.3 * i)
            scale = DIFF_HD ** -0.5 * LOG2_E
            q_l, k_l, vt_l = _qkv(x, mx, norm_g[i, 1], diff_w_qkv[j], scale, rope_tabs=_rope_tables(seq))
            q_c, k_c, vt_c = _qkv(xc, mc, norm_g[i, 1], diff_w_qkv[j], scale)
            o_l = _diff_attn(diff_lam[j], diff_subln_g[j], q_l, k_c, vt_c, k_l, vt_l, lam_init)
            tail_x = dict(proj=(o_l, diff_w_o[j], None))
            if update_ctx:
                o_c = _diff_attn(diff_lam[j], diff_subln_g[j], q_c, k_c, vt_c, None, None, lam_init)
                tail_c = dict(proj=(o_c, diff_w_o[j], None))
        else:
            scale = NAT_HD ** -0.5 * LOG2_E
            q_l, k_l, vt_l = _qkv(x, mx, norm_g[i, 1], nat_w_qkv[j], scale, b_qkv=nat_b_qkv[j])
            q_c, k_c, vt_c = _qkv(xc, mc, norm_g[i, 1], nat_w_qkv[j], scale, b_qkv=nat_b_qkv[j])
            o_l = _nat_attn(q_l, k_c, vt_c, _nat_bias(nat_rpb[j]), k_l, vt_l)
            tail_x = dict(proj=(o_l, nat_w_o[j], nat_b_o[j]))
            if update_ctx:
                o_c = _nat_attn(q_c, k_c, vt_c)
                tail_c = dict(proj=(o_c, nat_w_o[j], nat_b_o[j]))

        x = _ffn(x, mx, norm_g[i, 2], w_in, w_out, (i, 1), final_g=final_g if last else None, **tail_x)
        if update_ctx:
            xc = _ffn(xc, mc, norm_g[i, 2], w_in, w_out, (i, 1), **tail_c)
    return x
```

```python
import functools
import math

import numpy as np
import jax
import jax.numpy as jnp
from jax import lax
from jax.experimental import pallas as pl
from jax.experimental.pallas import tpu as pltpu

D_MODEL = 1024
DEPTH = 4
GRID_W = 64
N_MIXERS = 3
D_FF = 2816
N_MOD = 9
POOL_WINDOWS = (2, 4, 8, 16)
POOL_GW = 256
POOL_HALO = 8
POOL_SUBTILE = 256
DIFF_TILES_PER_STEP = 4
DIFF_HEADS = 8
DIFF_HD = 64
NAT_HEADS = 16
NAT_HD = 64
NAT_WIN_ROWS = 8
NAT_WIN_COLS = 16
NAT_QROWS = 4
NAT_BAND = 12
ROPE_THETA = 10000.0
NORM_EPS = 1e-6
NEG = -1e30
LOG2_E = math.log2(math.e)
BOUND_MARGIN = 1.01
FIXED_OFFSET_SPAN = 64.0

LANE = 128
FF_CHUNK = 256
VMEM_LIMIT = 56 * 1024 * 1024

F32 = jnp.float32
BF16 = jnp.bfloat16


def _cparams(n_axes):
    return pltpu.CompilerParams(dimension_semantics=("arbitrary",) * n_axes, vmem_limit_bytes=VMEM_LIMIT)


def _const_spec(shape):
    nd = len(shape)
    return pl.BlockSpec(shape, lambda *_: (0,) * nd, pipeline_mode=pl.Buffered(1))


def _mod_spec(mod):
    if mod.shape[0] == 1:
        return pl.BlockSpec((1, N_MOD, D_MODEL), lambda b, i: (0, 0, 0))
    return pl.BlockSpec((1, N_MOD, D_MODEL), lambda b, i: (b, 0, 0))


def _sigmoid(x):
    return 1.0 / (1.0 + jnp.exp(-x))


def _rms(x):
    return x * lax.rsqrt(jnp.mean(x * x, axis=-1, keepdims=True) + NORM_EPS)


def _norm_mod(x, g, mod_ref, row):
    shift = mod_ref[0, row:row + 1, :]
    scale = mod_ref[0, row + 1:row + 2, :]
    return (_rms(x) * g) * (1.0 + scale) + shift


def _ada_kernel(c_ref, w_ref, b_ref, o_ref):
    c = c_ref[...]
    s = (c * _sigmoid(c)).astype(BF16)
    o_ref[0] = jnp.dot(s, w_ref[0].astype(BF16), preferred_element_type=F32) + b_ref[0]


def _ada_mods(c, c_ctx, ada_w, ada_b):
    d = D_MODEL
    cc = jnp.concatenate([c, c_ctx[None], jnp.zeros((3, d), F32)], axis=0)
    tn = 1024
    return pl.pallas_call(
        _ada_kernel,
        out_shape=jax.ShapeDtypeStruct((DEPTH, 8, N_MOD * d), F32),
        grid=(DEPTH, N_MOD * d // tn),
        in_specs=[pl.BlockSpec((8, d), lambda l, n: (0, 0)),
                  pl.BlockSpec((1, d, tn), lambda l, n: (l, 0, n)),
                  pl.BlockSpec((1, 1, tn), lambda l, n: (l, 0, n))],
        out_specs=pl.BlockSpec((1, 8, tn), lambda l, n: (l, 0, n)),
        compiler_params=_cparams(2), name="ada_mods",
    )(cc, ada_w, ada_b.reshape(DEPTH, 1, N_MOD * d))


def _swiglu_half_step(x, mod_ref, g_ref, win_ref, wout_ref, a_ref, row, r0=0, side_work=()):
    rows = slice(r0, r0 + x.shape[0])
    side_work = list(side_work)
    h = _norm_mod(x, g_ref[...], mod_ref, row).astype(BF16)
    for j in range(D_FF // FF_CHUNK):
        lo = j * FF_CHUNK
        g = jnp.dot(h, win_ref[0, 0, :, lo:lo + FF_CHUNK], preferred_element_type=F32)
        u = jnp.dot(h, win_ref[0, 0, :, D_FF + lo:D_FF + lo + FF_CHUNK], preferred_element_type=F32)
        a_ref[rows, lo:lo + FF_CHUNK] = ((g * _sigmoid(g)) * u).astype(BF16)
        if side_work and j % 2 == 1:
            side_work.pop(0)()
    for thunk in side_work:
        thunk()
    y = jnp.dot(a_ref[rows, :], wout_ref[0, 0], preferred_element_type=F32)
    return x + (0.5 * mod_ref[0, row + 2:row + 3, :]) * y


def _pool_fill(x_ref, xp_ref, xn_ref, mod_ref, g_ref, ext_ref):
    i = pl.program_id(1)
    last = pl.num_programs(1) - 1
    tm = x_ref.shape[1]
    g = g_ref[...]
    ext_ref[0:POOL_HALO, :] = jnp.where(i > 0, _norm_mod(xp_ref[0], g, mod_ref, 3), 0.0)
    ext_ref[POOL_HALO:POOL_HALO + tm, :] = _norm_mod(x_ref[0], g, mod_ref, 3)
    ext_ref[POOL_HALO + tm:, :] = jnp.where(i < last, _norm_mod(xn_ref[0], g, mod_ref, 3), 0.0)


def _pool_group(x_ref, mod_ref, pw_ref, ps_ref, ext_ref, seq, r0, n, gi):
    tm = x_ref.shape[1]
    win = POOL_WINDOWS[gi]
    t = pl.program_id(1) * tm + r0 + lax.broadcasted_iota(jnp.int32, (n, 1), 0)
    base = POOL_HALO + r0
    cols = slice(gi * POOL_GW, (gi + 1) * POOL_GW)
    lo_off = -(win // 2)
    acc = ext_ref[base + lo_off:base + lo_off + n, cols]
    for o in range(lo_off + 1, lo_off + win):
        acc = acc + ext_ref[base + o:base + o + n, cols]
    lo = jnp.clip(t + lo_off, 0, seq - 1)
    hi = jnp.clip(t + lo_off + win - 1, 0, seq - 1)
    cnt = (hi - lo + 1).astype(F32)
    diff = (acc / cnt - ext_ref[base:base + n, cols]).astype(BF16)
    y = jnp.dot(diff, pw_ref[gi], preferred_element_type=F32) * ps_ref[:, cols]
    return x_ref[0, r0:r0 + n, cols] + mod_ref[0, 5:6, cols] * y


def _ffn_kernel(*refs, mixer, seq, bias, final):
    refs = list(refs)
    x_ref = refs.pop(0)
    if mixer == "pool":
        xp_ref, xn_ref = refs.pop(0), refs.pop(0)
    elif mixer == "proj":
        attn_ref = refs.pop(0)
    mod_ref = refs.pop(0)
    if mixer == "pool":
        g1_ref, pw_ref, ps_ref = refs.pop(0), refs.pop(0), refs.pop(0)
    elif mixer == "proj":
        wo_ref = refs.pop(0)
        bo_ref = refs.pop(0) if bias else None
    g_ref, win_ref, wout_ref = refs.pop(0), refs.pop(0), refs.pop(0)
    fg_ref = refs.pop(0) if final else None
    o_ref, a_ref = refs.pop(0), refs.pop(0)
    tm = x_ref.shape[1]

    def finish(out):
        return _rms(out) * fg_ref[...] if final else out

    if mixer == "pool":
        ext_ref = refs.pop(0)
        n = min(tm, POOL_SUBTILE)
        n_groups = len(POOL_WINDOWS)
        _pool_fill(x_ref, xp_ref, xn_ref, mod_ref, g1_ref, ext_ref)
        group = functools.partial(_pool_group, x_ref, mod_ref, pw_ref, ps_ref, ext_ref, seq)
        cur = [group(0, n, gi) for gi in range(n_groups)]
        for r0 in range(0, tm, n):
            nxt, side = [], []
            if r0 + n < tm:
                side = [lambda gi=gi, r=r0 + n: nxt.append(group(r, n, gi)) for gi in range(n_groups)]
            out = _swiglu_half_step(jnp.concatenate(cur, axis=1), mod_ref, g_ref, win_ref, wout_ref, a_ref, 6, r0, side)
            o_ref[0, r0:r0 + n, :] = finish(out)
            cur = nxt
        return
    if mixer == "proj":
        y = jnp.dot(attn_ref[0], wo_ref[...], preferred_element_type=F32)
        if bias:
            y = y + bo_ref[...]
        x = x_ref[0] + mod_ref[0, 5:6, :] * y
        row = 6
    else:
        x = x_ref[0]
        row = 0
    o_ref[0] = finish(_swiglu_half_step(x, mod_ref, g_ref, win_ref, wout_ref, a_ref, row))


def _ffn(x, mod, g, w_in, w_out, widx, pool=None, proj=None, final_g=None):
    bx, sx, d = x.shape
    tm = min(512, sx)
    tok = pl.BlockSpec((1, tm, d), lambda b, i: (b, i, 0))
    in_specs, args, scratch = [tok], [x], [pltpu.VMEM((tm, D_FF), BF16)]
    mixer = None
    if pool is not None:
        mixer = "pool"
        nb, n8 = tm // POOL_HALO, sx // POOL_HALO
        in_specs += [pl.BlockSpec((1, POOL_HALO, d), lambda b, i: (b, jnp.maximum(i * nb - 1, 0), 0)),
                     pl.BlockSpec((1, POOL_HALO, d), lambda b, i: (b, jnp.minimum((i + 1) * nb, n8 - 1), 0))]
        args += [x, x]
        scratch.append(pltpu.VMEM((tm + 2 * POOL_HALO, d), F32))
    elif proj is not None:
        mixer = "proj"
        in_specs.append(tok)
        args.append(proj[0])
    in_specs.append(_mod_spec(mod))
    args.append(mod)
    if pool is not None:
        in_specs += [_const_spec((1, d)), _const_spec((len(POOL_WINDOWS), POOL_GW, POOL_GW)), _const_spec((1, d))]
        args += [pool[0].reshape(1, d), pool[1], pool[2].reshape(1, d)]
    elif proj is not None:
        in_specs.append(_const_spec((d, d)))
        args.append(proj[1].astype(BF16))
        if proj[2] is not None:
            in_specs.append(_const_spec((1, d)))
            args.append(proj[2].reshape(1, d))
    in_specs += [_const_spec((1, d)),
                 pl.BlockSpec((1, 1, d, 2 * D_FF), lambda b, i: widx + (0, 0), pipeline_mode=pl.Buffered(1)),
                 pl.BlockSpec((1, 1, D_FF, d), lambda b, i: widx + (0, 0), pipeline_mode=pl.Buffered(1))]
    args += [g.reshape(1, d), w_in, w_out]
    if final_g is not None:
        in_specs.append(_const_spec((1, d)))
        args.append(final_g.reshape(1, d))
    return pl.pallas_call(
        functools.partial(_ffn_kernel, mixer=mixer, seq=sx, bias=proj is not None and proj[2] is not None,
                          final=final_g is not None),
        out_shape=jax.ShapeDtypeStruct(x.shape, F32),
        grid=(bx, sx // tm),
        in_specs=in_specs,
        out_specs=tok,
        scratch_shapes=scratch,
        compiler_params=_cparams(2), name="ffn" if mixer is None else mixer + "_ffn",
    )(*args)


def _rope_table_kernel(inv_ref, cos_ref, sin_ref):
    tm = cos_ref.shape[0]
    t = pl.program_id(0) * tm + lax.broadcasted_iota(jnp.int32, (tm, LANE), 0)
    lane = lax.broadcasted_iota(jnp.int32, (tm, LANE), 1)
    pos = jnp.where((lane & 63) < 32, lax.shift_right_logical(t, 6), t & (GRID_W - 1)).astype(F32)
    ang = pos * inv_ref[...]
    cos_ref[...] = jnp.cos(ang)
    sin_ref[...] = jnp.where((lane & 31) < 16, -jnp.sin(ang), jnp.sin(ang))


def _rope_tables(seq):
    per_axis = DIFF_HD // 2
    inv = ROPE_THETA ** (-jnp.arange(0, per_axis, 2, dtype=F32) / per_axis)
    inv_lane = jnp.tile(inv, LANE // inv.shape[0]).reshape(1, LANE)
    tm = 1024
    return pl.pallas_call(
        _rope_table_kernel,
        out_shape=(jax.ShapeDtypeStruct((seq, LANE), F32),) * 2,
        grid=(seq // tm,),
        in_specs=[pl.BlockSpec((1, LANE), lambda i: (0, 0))],
        out_specs=(pl.BlockSpec((tm, LANE), lambda i: (i, 0)),) * 2,
        compiler_params=_cparams(1), name="rope_tables",
    )(inv_lane)


def _qkv_kernel(x_ref, mod_ref, g_ref, wqk_ref, wvt_ref, *rest, rope, bias, scale):
    rest = list(rest)
    if bias:
        bqk_ref, bvt_ref = rest[:2]
        rest = rest[2:]
    if rope:
        cos_ref, sin_ref = rest[:2]
        rest = rest[2:]
    q_ref, k_ref, vt_ref = rest
    d = x_ref.shape[2]
    h = _norm_mod(x_ref[0], g_ref[...], mod_ref, 3).astype(BF16)
    vt = lax.dot_general(wvt_ref[...], h, (((1,), (1,)), ((), ())), preferred_element_type=F32)
    if bias:
        vt = vt + bvt_ref[...]
    vt_ref[0] = vt.astype(BF16)
    if rope:
        cos = cos_ref[...]
        sin = sin_ref[...]
        lane = lax.broadcasted_iota(jnp.int32, cos.shape, 1)
        first_half = (lane & 31) < 16
    for cb in range(2 * d // FF_CHUNK):
        c0 = cb * FF_CHUNK
        t2 = jnp.dot(h, wqk_ref[:, c0:c0 + FF_CHUNK], preferred_element_type=F32)
        if bias:
            t2 = t2 + bqk_ref[:, c0:c0 + FF_CHUNK]
        for half in range(FF_CHUNK // LANE):
            t = t2[:, half * LANE:(half + 1) * LANE]
            if rope:
                partner = jnp.where(first_half, pltpu.roll(t, LANE - 16, 1), pltpu.roll(t, 16, 1))
                t = t * cos + partner * sin
            col = c0 + half * LANE
            if col < d:
                q_ref[0, :, col:col + LANE] = (t * scale).astype(BF16)
            else:
                k_ref[0, :, col - d:col - d + LANE] = t.astype(BF16)


def _qkv(x, mod, g, w_qkv, scale, b_qkv=None, rope_tabs=None):
    bx, sx, d = x.shape
    tm = min(512, sx)
    wqk = w_qkv[:, :2 * d].astype(BF16)
    wvt = w_qkv[:, 2 * d:].T.astype(BF16)
    in_specs = [pl.BlockSpec((1, tm, d), lambda b, i: (b, i, 0)), _mod_spec(mod), _const_spec((1, d)),
                _const_spec((d, 2 * d)), _const_spec((d, d))]
    args = [x, mod, g.reshape(1, d), wqk, wvt]
    if b_qkv is not None:
        in_specs += [_const_spec((1, 2 * d)), _const_spec((d, 1))]
        args += [b_qkv[:2 * d].reshape(1, 2 * d), b_qkv[2 * d:].reshape(d, 1)]
    if rope_tabs is not None:
        in_specs += [pl.BlockSpec((tm, LANE), lambda b, i: (i, 0))] * 2
        args += list(rope_tabs)
    tok = pl.BlockSpec((1, tm, d), lambda b, i: (b, i, 0))
    return pl.pallas_call(
        functools.partial(_qkv_kernel, rope=rope_tabs is not None, bias=b_qkv is not None, scale=scale),
        out_shape=(jax.ShapeDtypeStruct((bx, sx, d), BF16), jax.ShapeDtypeStruct((bx, sx, d), BF16),
                   jax.ShapeDtypeStruct((bx, d, sx), BF16)),
        grid=(bx, sx // tm),
        in_specs=in_specs,
        out_specs=(tok, tok, pl.BlockSpec((1, d, tm), lambda b, i: (b, 0, i))),
        compiler_params=_cparams(2), name="qkv_proj",
    )(*args)


def _subhead_sq_norms(x, ones):
    xf = x.astype(F32)
    return jnp.dot((xf * xf).astype(BF16), ones, preferred_element_type=F32)


def _diff_attn_kernel(lam_ref, g_ref, q_ref, kc_ref, vct_ref, *rest, lam_init, tk, tq):
    if len(rest) == 9:
        k_ref, vt_ref, o_ref, acc_ref, sa_ref, sb_ref, sc_ref, ml_ref, kn_ref = rest
    else:
        k_ref = vt_ref = None
        o_ref, acc_ref = rest
        sa_ref = sb_ref = sc_ref = ml_ref = kn_ref = None
    n_sub = q_ref.shape[1] // tq
    hw = 2 * DIFF_HD
    n_chunks = None

    if k_ref is not None:
        n_chunks = k_ref.shape[1] // tk

        @pl.when(pl.program_id(2) == 0)
        def _():
            r = lax.broadcasted_iota(jnp.int32, (hw, hw), 0)
            c_ = lax.broadcasted_iota(jnp.int32, (hw, hw), 1)
            same_subhead = jnp.where((r < DIFF_HD) == (c_ < DIFF_HD), 1.0, 0.0).astype(BF16)
            kn = jnp.max(_subhead_sq_norms(kc_ref[0], same_subhead), axis=0, keepdims=True)
            for c in range(n_chunks):
                kn = jnp.maximum(kn, jnp.max(_subhead_sq_norms(k_ref[0, c * tk:(c + 1) * tk, :], same_subhead),
                                             axis=0, keepdims=True))
            kn_ref[...] = kn

    def split_subheads(rows):
        q = q_ref[0, rows, :]
        lane = lax.broadcasted_iota(jnp.int32, q.shape, 1)
        zero = jnp.zeros_like(q)
        return q, jnp.concatenate([jnp.where(lane < DIFF_HD, q, zero), jnp.where(lane >= DIFF_HD, q, zero)], axis=0)

    def scores(k_t, qq):
        return lax.dot_general(k_t, qq, (((1,), (1,)), ((), ())), preferred_element_type=F32)

    def context_block(s, m):
        p = jnp.exp2(s - m)
        acc_ref[...] = jnp.dot(vct_ref[0], p.astype(BF16), preferred_element_type=F32)
        return jnp.sum(p, axis=0, keepdims=True)

    def finish(rows, l):
        lf = lam_ref[...]
        lam = (jnp.exp(jnp.sum(lf[0:1] * lf[1:2], axis=-1, keepdims=True))
               - jnp.exp(jnp.sum(lf[2:3] * lf[3:4], axis=-1, keepdims=True)) + lam_init)
        acc = acc_ref[...]
        inv = 1.0 / l
        o = acc[:, :tq] * inv[:, :tq] - lam * (acc[:, tq:] * inv[:, tq:])
        o = o * lax.rsqrt(jnp.mean(o * o, axis=0, keepdims=True) + NORM_EPS)
        o = (o * g_ref[...]) * (1.0 - lam_init)
        o_ref[0, rows, :] = o.T.astype(BF16)

    if k_ref is None:
        assert n_sub == 1
        rows = slice(0, tq)
        _, qq = split_subheads(rows)
        s = scores(kc_ref[0], qq)
        finish(rows, context_block(s, jnp.max(s, axis=0, keepdims=True)))
        return

    r = lax.broadcasted_iota(jnp.int32, (16, hw), 0)
    c_ = lax.broadcasted_iota(jnp.int32, (16, hw), 1)
    select = jnp.where(r == jnp.where(c_ < DIFF_HD, 0, 1), 1.0, 0.0).astype(BF16)
    kn = kn_ref[...]
    gap = None
    for t in range(n_sub):
        q, qq = split_subheads(slice(t * tq, (t + 1) * tq))
        s = scores(kc_ref[0], qq)
        m = jnp.max(s, axis=0, keepdims=True)
        qf = q.astype(F32)
        qn = lax.dot_general(select, (qf * qf).astype(BF16), (((1,), (1,)), ((), ())),
                             preferred_element_type=F32)
        bound = jnp.concatenate([jnp.sqrt(qn[0:1, :] * kn[:, 0:1]),
                                 jnp.sqrt(qn[1:2, :] * kn[:, DIFF_HD:DIFF_HD + 1])], axis=1) * BOUND_MARGIN
        sc_ref[t] = s
        ml_ref[t, 0:1, :] = m
        ml_ref[t, 1:2, :] = bound
        gap = bound - m if gap is None else jnp.maximum(gap, bound - m)
    fixed = jnp.max(gap) <= FIXED_OFFSET_SPAN

    def run_tiles(online):
        def produce(c, s_ref, qq):
            s = scores(k_ref[0, c * tk:(c + 1) * tk, :], qq)
            s_ref[...] = s
            return jnp.max(s, axis=0, keepdims=True) if online else None

        def consume(s_ref, c, s_max, m, l):
            if online:
                m_new = jnp.maximum(m, s_max)
                alpha = jnp.exp2(m - m_new)
                p = jnp.exp2(s_ref[...] - m_new)
                l = alpha * l + jnp.sum(p, axis=0, keepdims=True)
                pv_prev = alpha * acc_ref[...]
            else:
                p = jnp.exp2(s_ref[...] - m)
                l = l + jnp.sum(p, axis=0, keepdims=True)
                pv_prev = acc_ref[...]
                m_new = m
            vt_t = vt_ref[0, :, c * tk:(c + 1) * tk]
            acc_ref[...] = pv_prev + jnp.dot(vt_t, p.astype(BF16), preferred_element_type=F32)
            return m_new, l

        def tile(t, carry):
            rows = pl.ds(pl.multiple_of(t * tq, tq), tq)
            _, qq = split_subheads(rows)
            m = ml_ref[t, 0:1, :] if online else ml_ref[t, 1:2, :]
            bufs = (sa_ref, sb_ref)
            if not online:
                l = context_block(sc_ref[t], m)
                for c in range(n_chunks):
                    p = jnp.exp2(scores(k_ref[0, c * tk:(c + 1) * tk, :], qq) - m)
                    l = l + jnp.sum(p, axis=0, keepdims=True)
                    acc_ref[...] += jnp.dot(vt_ref[0, :, c * tk:(c + 1) * tk], p.astype(BF16),
                                            preferred_element_type=F32)
                finish(rows, l)
                return carry
            s_max = produce(0, bufs[0], qq)
            l = context_block(sc_ref[t], m)
            for c in range(n_chunks):
                if c + 1 < n_chunks:
                    next_max = produce(c + 1, bufs[(c + 1) % 2], qq)
                m, l = consume(bufs[c % 2], c, s_max, m, l)
                s_max = next_max
            finish(rows, l)
            return carry

        lax.fori_loop(0, n_sub, tile, 0)

    pl.when(fixed)(functools.partial(run_tiles, False))
    pl.when(jnp.logical_not(fixed))(functools.partial(run_tiles, True))


def _diff_attn(lam, subln_g, q, kc, vct, k, vt, lam_init):
    bq, sq, d = q.shape
    n_ctx = kc.shape[1]
    tq = 256
    tq_step = tq * math.gcd(sq // tq, DIFF_TILES_PER_STEP)
    hw = 2 * DIFF_HD
    in_specs = [_const_spec((4, DIFF_HD)), _const_spec((hw, 1)),
                pl.BlockSpec((1, tq_step, hw), lambda b, h, i: (b, i, h)),
                pl.BlockSpec((1, n_ctx, hw), lambda b, h, i: (b, 0, h)),
                pl.BlockSpec((1, hw, n_ctx), lambda b, h, i: (b, h, 0))]
    args = [lam, subln_g.reshape(hw, 1), q, kc, vct]
    scratch = [pltpu.VMEM((hw, 2 * tq), F32)]
    tk = 1024
    if k is not None:
        s = k.shape[1]
        assert s % tk == 0
        in_specs += [pl.BlockSpec((1, s, hw), lambda b, h, i: (b, 0, h)),
                     pl.BlockSpec((1, hw, s), lambda b, h, i: (b, h, 0))]
        args += [k, vt]
        n_sub = tq_step // tq
        scratch += [pltpu.VMEM((tk, 2 * tq), F32)] * 2 + [pltpu.VMEM((n_sub, n_ctx, 2 * tq), F32),
                                                          pltpu.VMEM((n_sub, 2, 2 * tq), F32), pltpu.VMEM((1, hw), F32)]
    return pl.pallas_call(
        functools.partial(_diff_attn_kernel, lam_init=lam_init, tk=tk, tq=tq),
        out_shape=jax.ShapeDtypeStruct((bq, sq, d), BF16),
        grid=(bq, DIFF_HEADS, sq // tq_step),
        in_specs=in_specs,
        out_specs=pl.BlockSpec((1, tq_step, hw), lambda b, h, i: (b, i, h)),
        scratch_shapes=scratch,
        compiler_params=_cparams(3), name="diff_attn",
    )(*args)


def _nat_bias_rows():
    table = []
    for v, shift in enumerate((0, 4, 8)):
        rows = []
        for i in range(NAT_BAND):
            per_d = []
            for dq in range(NAT_QROWS):
                first = (0, dq, 4)[v]
                ok = first <= i < first + NAT_WIN_ROWS
                per_d.append(i - shift - dq + NAT_WIN_ROWS - 1 if ok else None)
            rows.append(per_d)
        table.append(rows)
    return table


def _nat_toeplitz_kernel(r_ref, o_ref):
    m, n = o_ref.shape

    def window(shape):
        col = lax.broadcasted_iota(jnp.int32, shape, 1)
        kc = lax.shift_right_logical(col, 7)
        qc = col & (GRID_W - 1)
        cs = jnp.clip(qc - NAT_WIN_COLS // 2, 0, GRID_W - NAT_WIN_COLS)
        return jnp.where(kc >= cs, jnp.where(kc < cs + NAT_WIN_COLS, kc - qc + NAT_WIN_COLS - 1, -1), -1)

    j = lax.broadcasted_iota(jnp.int32, (LANE, n), 0)
    onehot = jnp.where(j == window((LANE, n)), 1.0, 0.0).astype(BF16)
    r = r_ref[...]
    r_hi = r.astype(BF16)
    r1 = r - r_hi.astype(F32)
    r_mid = r1.astype(BF16)
    r_lo = (r1 - r_mid.astype(F32)).astype(BF16)
    t = (jnp.dot(r_hi, onehot, preferred_element_type=F32) + jnp.dot(r_mid, onehot, preferred_element_type=F32)
         + jnp.dot(r_lo, onehot, preferred_element_type=F32))
    o_ref[...] = jnp.where(window((m, n)) >= 0, t * LOG2_E, NEG)


def _nat_bias_kernel(t_ref, o_ref):
    lane = lax.broadcasted_iota(jnp.int32, (GRID_W, LANE), 1)
    neg = jnp.full((GRID_W, LANE), NEG, F32)
    rows = _nat_bias_rows()
    for v in range(3):
        for i in range(NAT_BAND):
            for pair in range(NAT_QROWS // 2):
                a_l, a_r = rows[v][i][2 * pair], rows[v][i][2 * pair + 1]
                left = neg if a_l is None else t_ref[0, a_l]
                right = neg if a_r is None else t_ref[0, a_r]
                o_ref[0, v, i * GRID_W:(i + 1) * GRID_W, pair * LANE:(pair + 1) * LANE] = (
                    jnp.where(lane < GRID_W, left, right))


def _nat_bias(rpb):
    nh, nr, nc = rpb.shape
    r2 = jnp.pad(rpb, ((0, 0), (0, 16 - nr), (0, LANE - nc))).reshape(nh * 16, LANE)
    n = GRID_W * LANE
    t = pl.pallas_call(
        _nat_toeplitz_kernel,
        out_shape=jax.ShapeDtypeStruct((nh * 16, n), F32),
        grid=(1,),
        in_specs=[pl.BlockSpec((nh * 16, LANE), lambda i: (0, 0))],
        out_specs=pl.BlockSpec((nh * 16, n), lambda i: (0, 0)),
        compiler_params=_cparams(1), name="nat_toeplitz",
    )(r2)
    t = t.reshape(nh, 16, GRID_W, LANE)
    nk, nq = NAT_BAND * GRID_W, NAT_QROWS * GRID_W
    return pl.pallas_call(
        _nat_bias_kernel,
        out_shape=jax.ShapeDtypeStruct((nh, 3, nk, nq), F32),
        grid=(nh,),
        in_specs=[pl.BlockSpec((1, 16, GRID_W, LANE), lambda h: (h, 0, 0, 0))],
        out_specs=pl.BlockSpec((1, 3, nk, nq), lambda h: (h, 0, 0, 0)),
        compiler_params=_cparams(1), name="nat_bias",
    )(t)


def _nat_attn_kernel(q_ref, kc_ref, vct_ref, *rest, n_rows, n_blk, group):
    if len(rest) == 6:
        bias_ref, k_ref, vt_ref, o_ref, sa_ref, sb_ref = rest
    else:
        bias_ref = k_ref = vt_ref = None
        o_ref, sa_ref, sb_ref = rest
    tq = NAT_QROWS * GRID_W
    nk = NAT_BAND * GRID_W
    n_ctx = kc_ref.shape[1]
    n_heads = LANE // NAT_HD
    dn = (((1,), (1,)), ((), ()))
    lane = lax.broadcasted_iota(jnp.int32, (tq, LANE), 1)
    zero = jnp.zeros((tq, LANE), BF16)

    def band(g):
        blk = pl.program_id(2) * group + g
        start = jnp.clip(blk * NAT_QROWS - NAT_WIN_ROWS // 2, 0, n_rows - NAT_BAND)
        return (pl.multiple_of(start * GRID_W, NAT_QROWS * GRID_W),
                jnp.where(blk == 0, 0, jnp.where(blk == n_blk - 1, 2, 1)))

    def produce(g, hh, s_ref):
        q = q_ref[0, g * tq:(g + 1) * tq, :]
        qh = jnp.where((lane >= hh * NAT_HD) & (lane < (hh + 1) * NAT_HD), q, zero)
        s_c = lax.dot_general(kc_ref[0], qh, dn, preferred_element_type=F32)
        s_ref[0:n_ctx, :] = s_c
        m = jnp.max(s_c, axis=0, keepdims=True)
        if k_ref is not None:
            off, variant = band(g)
            s_b = lax.dot_general(k_ref[0, pl.ds(off, nk), :], qh, dn, preferred_element_type=F32)
            s_b = s_b + bias_ref[hh, variant]
            s_ref[n_ctx:, :] = s_b
            m = jnp.maximum(m, jnp.max(s_b, axis=0, keepdims=True))
        return m

    def consume(g, hh, s_ref, m):
        rows = slice(hh * NAT_HD, (hh + 1) * NAT_HD)
        p = jnp.exp2(s_ref[...] - m)
        l = jnp.sum(p, axis=0, keepdims=True)
        pb = p.astype(BF16)
        o = jnp.dot(vct_ref[0, rows, :], pb[0:n_ctx], preferred_element_type=F32)
        if k_ref is not None:
            off, _ = band(g)
            o = o + jnp.dot(vt_ref[0, rows, pl.ds(off, nk)], pb[n_ctx:], preferred_element_type=F32)
        return o / l

    chains = [(g, hh) for g in range(group) for hh in range(n_heads)]
    bufs = (sa_ref, sb_ref)
    m = produce(*chains[0], bufs[0])
    outs = []
    for i, (g, hh) in enumerate(chains):
        if i + 1 < len(chains):
            m_next = produce(*chains[i + 1], bufs[(i + 1) % 2])
        outs.append(consume(g, hh, bufs[i % 2], m))
        m = m_next
        if hh == n_heads - 1:
            o_ref[0, g * tq:(g + 1) * tq, :] = jnp.concatenate(outs, axis=0).T.astype(BF16)
            outs = []


def _nat_attn(q, kc, vct, bias=None, k=None, vt=None):
    bq, sq, d = q.shape
    n_ctx = kc.shape[1]
    tq = NAT_QROWS * GRID_W
    n_blk = sq // tq
    group = math.gcd(n_blk, 4)
    in_specs = [pl.BlockSpec((1, group * tq, LANE), lambda hp, b, i: (b, i, hp)),
                pl.BlockSpec((1, n_ctx, LANE), lambda hp, b, i: (b, 0, hp)),
                pl.BlockSpec((1, LANE, n_ctx), lambda hp, b, i: (b, hp, 0))]
    args = [q, kc, vct]
    n_rows = None
    n_keys = n_ctx
    if k is not None:
        s = k.shape[1]
        n_rows = s // GRID_W
        nk = NAT_BAND * GRID_W
        n_keys += nk
        in_specs += [pl.BlockSpec((LANE // NAT_HD, 3, nk, tq), lambda hp, b, i: (hp, 0, 0, 0)),
                     pl.BlockSpec((1, s, LANE), lambda hp, b, i: (b, 0, hp)),
                     pl.BlockSpec((1, LANE, s), lambda hp, b, i: (b, hp, 0))]
        args += [bias, k, vt]
    return pl.pallas_call(
        functools.partial(_nat_attn_kernel, n_rows=n_rows, n_blk=n_blk, group=group),
        out_shape=jax.ShapeDtypeStruct((bq, sq, d), BF16),
        grid=(d // LANE, bq, n_blk // group),
        in_specs=in_specs,
        out_specs=pl.BlockSpec((1, group * tq, LANE), lambda hp, b, i: (b, i, hp)),
        scratch_shapes=[pltpu.VMEM((n_keys, tq), F32)] * 2,
        compiler_params=_cparams(3), name="nat_attn",
    )(*args)


def kernel(x, c, ctx, c_ctx, ada_w, ada_b, norm_g, ffn_w_in, ffn_w_out, pool_w, pool_scale,
           diff_w_qkv, diff_lam, diff_subln_g, diff_w_o, nat_w_qkv, nat_b_qkv, nat_rpb,
           nat_w_o, nat_b_o, final_g):
    b, seq, d = x.shape
    mods = _ada_mods(c, c_ctx, ada_w, ada_b)
    w_in = ffn_w_in.astype(BF16)
    w_out = ffn_w_out.astype(BF16)
    xc = ctx
    for i in range(DEPTH):
        kind = i % N_MIXERS
        j = i // N_MIXERS
        last = i == DEPTH - 1
        update_ctx = not last
        ctx_needed = update_ctx or kind != 0
        mx = mods[i, :b].reshape(b, N_MOD, d)
        mc = mods[i, b:b + 1].reshape(1, N_MOD, d)

        x = _ffn(x, mx, norm_g[i, 0], w_in, w_out, (i, 0))
        if ctx_needed:
            xc = _ffn(xc, mc, norm_g[i, 0], w_in, w_out, (i, 0))

        if kind == 0:
            tail_x = tail_c = dict(pool=(norm_g[i, 1], pool_w[j].astype(BF16), pool_scale[j]))
        elif kind == 1:
            lam_init = 0.8 - 0.6 * math.exp(-0.3 * i)
            scale = DIFF_HD ** -0.5 * LOG2_E
            q_l, k_l, vt_l = _qkv(x, mx, norm_g[i, 1], diff_w_qkv[j], scale, rope_tabs=_rope_tables(seq))
            q_c, k_c, vt_c = _qkv(xc, mc, norm_g[i, 1], diff_w_qkv[j], scale)
            o_l = _diff_attn(diff_lam[j], diff_subln_g[j], q_l, k_c, vt_c, k_l, vt_l, lam_init)
            tail_x = dict(proj=(o_l, diff_w_o[j], None))
            if update_ctx:
                o_c = _diff_attn(diff_lam[j], diff_subln_g[j], q_c, k_c, vt_c, None, None, lam_init)
                tail_c = dict(proj=(o_c, diff_w_o[j], None))
        else:
            scale = NAT_HD ** -0.5 * LOG2_E
            q_l, k_l, vt_l = _qkv(x, mx, norm_g[i, 1], nat_w_qkv[j], scale, b_qkv=nat_b_qkv[j])
            q_c, k_c, vt_c = _qkv(xc, mc, norm_g[i, 1], nat_w_qkv[j], scale, b_qkv=nat_b_qkv[j])
            o_l = _nat_attn(q_l, k_c, vt_c, _nat_bias(nat_rpb[j]), k_l, vt_l)
            tail_x = dict(proj=(o_l, nat_w_o[j], nat_b_o[j]))
            if update_ctx:
                o_c = _nat_attn(q_c, k_c, vt_c)
                tail_c = dict(proj=(o_c, nat_w_o[j], nat_b_o[j]))

        x = _ffn(x, mx, norm_g[i, 2], w_in, w_out, (i, 1), final_g=final_g if last else None, **tail_x)
        if update_ctx:
            xc = _ffn(xc, mc, norm_g[i, 2], w_in, w_out, (i, 1), **tail_c)
    return x
```

```python
import functools
import math

import jax
import jax.numpy as jnp
from jax import lax
from jax.experimental import pallas as pl
from jax.experimental.pallas import tpu as pltpu

D_MODEL = 1024
DEPTH = 4
GRID_W = 64
N_MIXERS = 3
D_FF = 2816
N_MOD = 9
POOL_WINDOWS = (2, 4, 8, 16)
POOL_GW = 256
POOL_HALO = 8
POOL_SUBTILE = 256

TOKEN_TILE = 512
WIDE_TILE = 1024
DIFF_TQ = 256
DIFF_TK = 1024
DIFF_TILES_PER_STEP = 4
NAT_BLOCKS_PER_STEP = 16
DIFF_HEADS = 8
DIFF_HD = 64
NAT_HEADS = 16
NAT_HD = 64
NAT_WIN_ROWS = 8
NAT_WIN_COLS = 16
NAT_QROWS = 4
NAT_BAND = 12
ROPE_THETA = 10000.0
NORM_EPS = 1e-6
NEG = -1e30
LOG2_E = math.log2(math.e)
BOUND_MARGIN = 1.01
FIXED_OFFSET_SPAN = 64.0

LANE = 128
FF_CHUNK = 256
VMEM_LIMIT = 56 * 1024 * 1024

F32 = jnp.float32
BF16 = jnp.bfloat16


def _cparams(n_axes):
    return pltpu.CompilerParams(dimension_semantics=("arbitrary",) * n_axes, vmem_limit_bytes=VMEM_LIMIT)


def _const_spec(shape):
    nd = len(shape)
    return pl.BlockSpec(shape, lambda *_: (0,) * nd, pipeline_mode=pl.Buffered(1))


def _mod_spec(mod):
    if mod.shape[0] == 1:
        return pl.BlockSpec((1, N_MOD, D_MODEL), lambda b, i: (0, 0, 0))
    return pl.BlockSpec((1, N_MOD, D_MODEL), lambda b, i: (b, 0, 0))


def _sigmoid(x):
    return 1.0 / (1.0 + jnp.exp(-x))


def _rms(x):
    return x * lax.rsqrt(jnp.mean(x * x, axis=-1, keepdims=True) + NORM_EPS)


def _norm_mod(x, g, mod_ref, row):
    shift = mod_ref[0, row:row + 1, :]
    scale = mod_ref[0, row + 1:row + 2, :]
    return (_rms(x) * g) * (1.0 + scale) + shift


def _ada_kernel(c_ref, w_ref, b_ref, o_ref):
    c = c_ref[...]
    s = (c * _sigmoid(c)).astype(BF16)
    o_ref[0] = jnp.dot(s, w_ref[0].astype(BF16), preferred_element_type=F32) + b_ref[0]


def _ada_mods(c, c_ctx, ada_w, ada_b):
    d = D_MODEL
    cc = jnp.concatenate([c, c_ctx[None], jnp.zeros((3, d), F32)], axis=0)
    tn = WIDE_TILE
    return pl.pallas_call(
        _ada_kernel,
        out_shape=jax.ShapeDtypeStruct((DEPTH, 8, N_MOD * d), F32),
        grid=(DEPTH, N_MOD * d // tn),
        in_specs=[pl.BlockSpec((8, d), lambda l, n: (0, 0)),
                  pl.BlockSpec((1, d, tn), lambda l, n: (l, 0, n)),
                  pl.BlockSpec((1, 1, tn), lambda l, n: (l, 0, n))],
        out_specs=pl.BlockSpec((1, 8, tn), lambda l, n: (l, 0, n)),
        compiler_params=_cparams(2), name="ada_mods",
    )(cc, ada_w, ada_b.reshape(DEPTH, 1, N_MOD * d))


def _swiglu_half_step(x, mod_ref, g_ref, win_ref, wout_ref, a_ref, row, r0=0, side_work=()):
    rows = slice(r0, r0 + x.shape[0])
    side_work = list(side_work)
    h = _norm_mod(x, g_ref[...], mod_ref, row).astype(BF16)
    for j in range(D_FF // FF_CHUNK):
        lo = j * FF_CHUNK
        g = jnp.dot(h, win_ref[0, 0, :, lo:lo + FF_CHUNK], preferred_element_type=F32)
        u = jnp.dot(h, win_ref[0, 0, :, D_FF + lo:D_FF + lo + FF_CHUNK], preferred_element_type=F32)
        a_ref[rows, lo:lo + FF_CHUNK] = ((g * _sigmoid(g)) * u).astype(BF16)
        if side_work and j % 2 == 1:
            side_work.pop(0)()
    for thunk in side_work:
        thunk()
    y = jnp.dot(a_ref[rows, :], wout_ref[0, 0], preferred_element_type=F32)
    return x + (0.5 * mod_ref[0, row + 2:row + 3, :]) * y


def _pool_fill(x_ref, xp_ref, xn_ref, mod_ref, g_ref, ext_ref):
    i = pl.program_id(1)
    last = pl.num_programs(1) - 1
    tm = x_ref.shape[1]
    g = g_ref[...]
    ext_ref[0:POOL_HALO, :] = jnp.where(i > 0, _norm_mod(xp_ref[0], g, mod_ref, 3), 0.0)
    ext_ref[POOL_HALO:POOL_HALO + tm, :] = _norm_mod(x_ref[0], g, mod_ref, 3)
    ext_ref[POOL_HALO + tm:, :] = jnp.where(i < last, _norm_mod(xn_ref[0], g, mod_ref, 3), 0.0)


def _pool_group(x_ref, mod_ref, pw_ref, ps_ref, ext_ref, seq, r0, n, gi):
    tm = x_ref.shape[1]
    win = POOL_WINDOWS[gi]
    t = pl.program_id(1) * tm + r0 + lax.broadcasted_iota(jnp.int32, (n, 1), 0)
    base = POOL_HALO + r0
    cols = slice(gi * POOL_GW, (gi + 1) * POOL_GW)
    lo_off = -(win // 2)
    acc = ext_ref[base + lo_off:base + lo_off + n, cols]
    for o in range(lo_off + 1, lo_off + win):
        acc = acc + ext_ref[base + o:base + o + n, cols]
    lo = jnp.clip(t + lo_off, 0, seq - 1)
    hi = jnp.clip(t + lo_off + win - 1, 0, seq - 1)
    cnt = (hi - lo + 1).astype(F32)
    diff = (acc / cnt - ext_ref[base:base + n, cols]).astype(BF16)
    y = jnp.dot(diff, pw_ref[gi], preferred_element_type=F32) * ps_ref[:, cols]
    return x_ref[0, r0:r0 + n, cols] + mod_ref[0, 5:6, cols] * y


def _ffn_kernel(*refs, mixer, seq, bias, final):
    refs = list(refs)
    x_ref = refs.pop(0)
    if mixer == "pool":
        xp_ref, xn_ref = refs.pop(0), refs.pop(0)
    elif mixer == "proj":
        attn_ref = refs.pop(0)
    mod_ref = refs.pop(0)
    if mixer == "pool":
        g1_ref, pw_ref, ps_ref = refs.pop(0), refs.pop(0), refs.pop(0)
    elif mixer == "proj":
        wo_ref = refs.pop(0)
        bo_ref = refs.pop(0) if bias else None
    g_ref, win_ref, wout_ref = refs.pop(0), refs.pop(0), refs.pop(0)
    fg_ref = refs.pop(0) if final else None
    o_ref, a_ref = refs.pop(0), refs.pop(0)
    tm = x_ref.shape[1]

    def finish(out):
        return _rms(out) * fg_ref[...] if final else out

    if mixer == "pool":
        ext_ref = refs.pop(0)
        n = min(tm, POOL_SUBTILE)
        n_groups = len(POOL_WINDOWS)
        _pool_fill(x_ref, xp_ref, xn_ref, mod_ref, g1_ref, ext_ref)
        group = functools.partial(_pool_group, x_ref, mod_ref, pw_ref, ps_ref, ext_ref, seq)
        cur = [group(0, n, gi) for gi in range(n_groups)]
        for r0 in range(0, tm, n):
            nxt, side = [], []
            if r0 + n < tm:
                side = [lambda gi=gi, r=r0 + n: nxt.append(group(r, n, gi)) for gi in range(n_groups)]
            out = _swiglu_half_step(jnp.concatenate(cur, axis=1), mod_ref, g_ref, win_ref, wout_ref, a_ref, 6, r0, side)
            o_ref[0, r0:r0 + n, :] = finish(out)
            cur = nxt
        return
    if mixer == "proj":
        y = jnp.dot(attn_ref[0], wo_ref[...], preferred_element_type=F32)
        if bias:
            y = y + bo_ref[...]
        x = x_ref[0] + mod_ref[0, 5:6, :] * y
        row = 6
    else:
        x = x_ref[0]
        row = 0
    o_ref[0] = finish(_swiglu_half_step(x, mod_ref, g_ref, win_ref, wout_ref, a_ref, row))


def _ffn(x, mod, g, w_in, w_out, widx, pool=None, proj=None, final_g=None):
    bx, sx, d = x.shape
    tm = min(TOKEN_TILE, sx)
    tok = pl.BlockSpec((1, tm, d), lambda b, i: (b, i, 0))
    in_specs, args, scratch = [tok], [x], [pltpu.VMEM((tm, D_FF), BF16)]
    mixer = None
    if pool is not None:
        mixer = "pool"
        nb, n8 = tm // POOL_HALO, sx // POOL_HALO
        in_specs += [pl.BlockSpec((1, POOL_HALO, d), lambda b, i: (b, jnp.maximum(i * nb - 1, 0), 0)),
                     pl.BlockSpec((1, POOL_HALO, d), lambda b, i: (b, jnp.minimum((i + 1) * nb, n8 - 1), 0))]
        args += [x, x]
        scratch.append(pltpu.VMEM((tm + 2 * POOL_HALO, d), F32))
    elif proj is not None:
        mixer = "proj"
        in_specs.append(tok)
        args.append(proj[0])
    in_specs.append(_mod_spec(mod))
    args.append(mod)
    if pool is not None:
        in_specs += [_const_spec((1, d)), _const_spec((len(POOL_WINDOWS), POOL_GW, POOL_GW)), _const_spec((1, d))]
        args += [pool[0].reshape(1, d), pool[1], pool[2].reshape(1, d)]
    elif proj is not None:
        in_specs.append(_const_spec((d, d)))
        args.append(proj[1].astype(BF16))
        if proj[2] is not None:
            in_specs.append(_const_spec((1, d)))
            args.append(proj[2].reshape(1, d))
    in_specs += [_const_spec((1, d)),
                 pl.BlockSpec((1, 1, d, 2 * D_FF), lambda b, i: widx + (0, 0), pipeline_mode=pl.Buffered(1)),
                 pl.BlockSpec((1, 1, D_FF, d), lambda b, i: widx + (0, 0), pipeline_mode=pl.Buffered(1))]
    args += [g.reshape(1, d), w_in, w_out]
    if final_g is not None:
        in_specs.append(_const_spec((1, d)))
        args.append(final_g.reshape(1, d))
    return pl.pallas_call(
        functools.partial(_ffn_kernel, mixer=mixer, seq=sx, bias=proj is not None and proj[2] is not None,
                          final=final_g is not None),
        out_shape=jax.ShapeDtypeStruct(x.shape, F32),
        grid=(bx, sx // tm),
        in_specs=in_specs,
        out_specs=tok,
        scratch_shapes=scratch,
        compiler_params=_cparams(2), name="ffn" if mixer is None else mixer + "_ffn",
    )(*args)


def _rope_table_kernel(inv_ref, cos_ref, sin_ref):
    tm = cos_ref.shape[0]
    t = pl.program_id(0) * tm + lax.broadcasted_iota(jnp.int32, (tm, LANE), 0)
    lane = lax.broadcasted_iota(jnp.int32, (tm, LANE), 1)
    pos = jnp.where((lane & 63) < 32, lax.shift_right_logical(t, 6), t & (GRID_W - 1)).astype(F32)
    ang = pos * inv_ref[...]
    cos_ref[...] = jnp.cos(ang)
    sin_ref[...] = jnp.where((lane & 31) < 16, -jnp.sin(ang), jnp.sin(ang))


def _rope_tables(seq):
    per_axis = DIFF_HD // 2
    inv = ROPE_THETA ** (-jnp.arange(0, per_axis, 2, dtype=F32) / per_axis)
    inv_lane = jnp.tile(inv, LANE // inv.shape[0]).reshape(1, LANE)
    tm = min(WIDE_TILE, seq)
    return pl.pallas_call(
        _rope_table_kernel,
        out_shape=(jax.ShapeDtypeStruct((seq, LANE), F32),) * 2,
        grid=(seq // tm,),
        in_specs=[pl.BlockSpec((1, LANE), lambda i: (0, 0))],
        out_specs=(pl.BlockSpec((tm, LANE), lambda i: (i, 0)),) * 2,
        compiler_params=_cparams(1), name="rope_tables",
    )(inv_lane)


def _qkv_kernel(x_ref, mod_ref, g_ref, wqk_ref, wvt_ref, *rest, rope, bias, scale):
    rest = list(rest)
    if bias:
        bqk_ref, bvt_ref = rest[:2]
        rest = rest[2:]
    if rope:
        cos_ref, sin_ref = rest[:2]
        rest = rest[2:]
    q_ref, k_ref, vt_ref = rest
    d = x_ref.shape[2]
    h = _norm_mod(x_ref[0], g_ref[...], mod_ref, 3).astype(BF16)
    vt = lax.dot_general(wvt_ref[...], h, (((1,), (1,)), ((), ())), preferred_element_type=F32)
    if bias:
        vt = vt + bvt_ref[...]
    vt_ref[0] = vt.astype(BF16)
    if rope:
        cos = cos_ref[...]
        sin = sin_ref[...]
        lane = lax.broadcasted_iota(jnp.int32, cos.shape, 1)
        first_half = (lane & 31) < 16
    for cb in range(2 * d // FF_CHUNK):
        c0 = cb * FF_CHUNK
        t2 = jnp.dot(h, wqk_ref[:, c0:c0 + FF_CHUNK], preferred_element_type=F32)
        if bias:
            t2 = t2 + bqk_ref[:, c0:c0 + FF_CHUNK]
        for half in range(FF_CHUNK // LANE):
            t = t2[:, half * LANE:(half + 1) * LANE]
            if rope:
                partner = jnp.where(first_half, pltpu.roll(t, LANE - 16, 1), pltpu.roll(t, 16, 1))
                t = t * cos + partner * sin
            col = c0 + half * LANE
            if col < d:
                q_ref[0, :, col:col + LANE] = (t * scale).astype(BF16)
            else:
                k_ref[0, :, col - d:col - d + LANE] = t.astype(BF16)


def _qkv(x, mod, g, w_qkv, scale, b_qkv=None, rope_tabs=None):
    bx, sx, d = x.shape
    tm = min(TOKEN_TILE, sx)
    wqk = w_qkv[:, :2 * d].astype(BF16)
    wvt = w_qkv[:, 2 * d:].T.astype(BF16)
    in_specs = [pl.BlockSpec((1, tm, d), lambda b, i: (b, i, 0)), _mod_spec(mod), _const_spec((1, d)),
                _const_spec((d, 2 * d)), _const_spec((d, d))]
    args = [x, mod, g.reshape(1, d), wqk, wvt]
    if b_qkv is not None:
        in_specs += [_const_spec((1, 2 * d)), _const_spec((d, 1))]
        args += [b_qkv[:2 * d].reshape(1, 2 * d), b_qkv[2 * d:].reshape(d, 1)]
    if rope_tabs is not None:
        in_specs += [pl.BlockSpec((tm, LANE), lambda b, i: (i, 0))] * 2
        args += list(rope_tabs)
    tok = pl.BlockSpec((1, tm, d), lambda b, i: (b, i, 0))
    return pl.pallas_call(
        functools.partial(_qkv_kernel, rope=rope_tabs is not None, bias=b_qkv is not None, scale=scale),
        out_shape=(jax.ShapeDtypeStruct((bx, sx, d), BF16), jax.ShapeDtypeStruct((bx, sx, d), BF16),
                   jax.ShapeDtypeStruct((bx, d, sx), BF16)),
        grid=(bx, sx // tm),
        in_specs=in_specs,
        out_specs=(tok, tok, pl.BlockSpec((1, d, tm), lambda b, i: (b, 0, i))),
        compiler_params=_cparams(2), name="qkv_proj",
    )(*args)


def _subhead_sq_norms(x, ones):
    xf = x.astype(F32)
    return jnp.dot((xf * xf).astype(BF16), ones, preferred_element_type=F32)


def _diff_attn_kernel(lam_ref, g_ref, q_ref, kc_ref, vct_ref, *rest, lam_init, tk, tq):
    if len(rest) == 9:
        k_ref, vt_ref, o_ref, acc_ref, sa_ref, sb_ref, sc_ref, ml_ref, kn_ref = rest
    else:
        k_ref = vt_ref = None
        o_ref, acc_ref = rest
        sa_ref = sb_ref = sc_ref = ml_ref = kn_ref = None
    n_sub = q_ref.shape[1] // tq
    hw = 2 * DIFF_HD
    n_chunks = None

    if k_ref is not None:
        n_chunks = k_ref.shape[1] // tk

        @pl.when(pl.program_id(2) == 0)
        def _():
            r = lax.broadcasted_iota(jnp.int32, (hw, hw), 0)
            c_ = lax.broadcasted_iota(jnp.int32, (hw, hw), 1)
            same_subhead = jnp.where((r < DIFF_HD) == (c_ < DIFF_HD), 1.0, 0.0).astype(BF16)
            kn = jnp.max(_subhead_sq_norms(kc_ref[0], same_subhead), axis=0, keepdims=True)
            for c in range(n_chunks):
                kn = jnp.maximum(kn, jnp.max(_subhead_sq_norms(k_ref[0, c * tk:(c + 1) * tk, :], same_subhead),
                                             axis=0, keepdims=True))
            kn_ref[...] = kn

    def split_subheads(rows):
        q = q_ref[0, rows, :]
        lane = lax.broadcasted_iota(jnp.int32, q.shape, 1)
        zero = jnp.zeros_like(q)
        return q, jnp.concatenate([jnp.where(lane < DIFF_HD, q, zero), jnp.where(lane >= DIFF_HD, q, zero)], axis=0)

    def scores(k_t, qq):
        return lax.dot_general(k_t, qq, (((1,), (1,)), ((), ())), preferred_element_type=F32)

    def context_block(s, m):
        p = jnp.exp2(s - m)
        acc_ref[...] = jnp.dot(vct_ref[0], p.astype(BF16), preferred_element_type=F32)
        return jnp.sum(p, axis=0, keepdims=True)

    def finish(rows, l):
        lf = lam_ref[...]
        lam = (jnp.exp(jnp.sum(lf[0:1] * lf[1:2], axis=-1, keepdims=True))
               - jnp.exp(jnp.sum(lf[2:3] * lf[3:4], axis=-1, keepdims=True)) + lam_init)
        acc = acc_ref[...]
        inv = 1.0 / l
        o = acc[:, :tq] * inv[:, :tq] - lam * (acc[:, tq:] * inv[:, tq:])
        o = o * lax.rsqrt(jnp.mean(o * o, axis=0, keepdims=True) + NORM_EPS)
        o = (o * g_ref[...]) * (1.0 - lam_init)
        o_ref[0, rows, :] = o.T.astype(BF16)

    if k_ref is None:
        assert n_sub == 1
        rows = slice(0, tq)
        _, qq = split_subheads(rows)
        s = scores(kc_ref[0], qq)
        finish(rows, context_block(s, jnp.max(s, axis=0, keepdims=True)))
        return

    r = lax.broadcasted_iota(jnp.int32, (16, hw), 0)
    c_ = lax.broadcasted_iota(jnp.int32, (16, hw), 1)
    select = jnp.where(r == jnp.where(c_ < DIFF_HD, 0, 1), 1.0, 0.0).astype(BF16)
    kn = kn_ref[...]
    gap = None
    for t in range(n_sub):
        q, qq = split_subheads(slice(t * tq, (t + 1) * tq))
        s = scores(kc_ref[0], qq)
        m = jnp.max(s, axis=0, keepdims=True)
        qf = q.astype(F32)
        qn = lax.dot_general(select, (qf * qf).astype(BF16), (((1,), (1,)), ((), ())),
                             preferred_element_type=F32)
        bound = jnp.concatenate([jnp.sqrt(qn[0:1, :] * kn[:, 0:1]),
                                 jnp.sqrt(qn[1:2, :] * kn[:, DIFF_HD:DIFF_HD + 1])], axis=1) * BOUND_MARGIN
        sc_ref[t] = s
        ml_ref[t, 0:1, :] = m
        ml_ref[t, 1:2, :] = bound
        gap = bound - m if gap is None else jnp.maximum(gap, bound - m)
    fixed = jnp.max(gap) <= FIXED_OFFSET_SPAN

    def run_tiles(online):
        def produce(c, s_ref, qq):
            s = scores(k_ref[0, c * tk:(c + 1) * tk, :], qq)
            s_ref[...] = s
            return jnp.max(s, axis=0, keepdims=True) if online else None

        def consume(s_ref, c, s_max, m, l):
            if online:
                m_new = jnp.maximum(m, s_max)
                alpha = jnp.exp2(m - m_new)
                p = jnp.exp2(s_ref[...] - m_new)
                l = alpha * l + jnp.sum(p, axis=0, keepdims=True)
                pv_prev = alpha * acc_ref[...]
            else:
                p = jnp.exp2(s_ref[...] - m)
                l = l + jnp.sum(p, axis=0, keepdims=True)
                pv_prev = acc_ref[...]
                m_new = m
            vt_t = vt_ref[0, :, c * tk:(c + 1) * tk]
            acc_ref[...] = pv_prev + jnp.dot(vt_t, p.astype(BF16), preferred_element_type=F32)
            return m_new, l

        def tile(t, carry):
            rows = pl.ds(pl.multiple_of(t * tq, tq), tq)
            _, qq = split_subheads(rows)
            m = ml_ref[t, 0:1, :] if online else ml_ref[t, 1:2, :]
            bufs = (sa_ref, sb_ref)
            if not online:
                l = context_block(sc_ref[t], m)
                for c in range(n_chunks):
                    p = jnp.exp2(scores(k_ref[0, c * tk:(c + 1) * tk, :], qq) - m)
                    l = l + jnp.sum(p, axis=0, keepdims=True)
                    acc_ref[...] += jnp.dot(vt_ref[0, :, c * tk:(c + 1) * tk], p.astype(BF16),
                                            preferred_element_type=F32)
                finish(rows, l)
                return carry
            s_max = produce(0, bufs[0], qq)
            l = context_block(sc_ref[t], m)
            for c in range(n_chunks):
                if c + 1 < n_chunks:
                    next_max = produce(c + 1, bufs[(c + 1) % 2], qq)
                m, l = consume(bufs[c % 2], c, s_max, m, l)
                s_max = next_max
            finish(rows, l)
            return carry

        lax.fori_loop(0, n_sub, tile, 0)

    pl.when(fixed)(functools.partial(run_tiles, False))
    pl.when(jnp.logical_not(fixed))(functools.partial(run_tiles, True))


def _diff_attn(lam, subln_g, q, kc, vct, k, vt, lam_init):
    bq, sq, d = q.shape
    n_ctx = kc.shape[1]
    tq = DIFF_TQ
    tq_step = tq * math.gcd(sq // tq, DIFF_TILES_PER_STEP)
    hw = 2 * DIFF_HD
    in_specs = [_const_spec((4, DIFF_HD)), _const_spec((hw, 1)),
                pl.BlockSpec((1, tq_step, hw), lambda b, h, i: (b, i, h)),
                pl.BlockSpec((1, n_ctx, hw), lambda b, h, i: (b, 0, h)),
                pl.BlockSpec((1, hw, n_ctx), lambda b, h, i: (b, h, 0))]
    args = [lam, subln_g.reshape(hw, 1), q, kc, vct]
    scratch = [pltpu.VMEM((hw, 2 * tq), F32)]
    tk = DIFF_TK
    if k is not None:
        s = k.shape[1]
        assert s % tk == 0
        in_specs += [pl.BlockSpec((1, s, hw), lambda b, h, i: (b, 0, h)),
                     pl.BlockSpec((1, hw, s), lambda b, h, i: (b, h, 0))]
        args += [k, vt]
        n_sub = tq_step // tq
        scratch += [pltpu.VMEM((tk, 2 * tq), F32)] * 2 + [pltpu.VMEM((n_sub, n_ctx, 2 * tq), F32),
                                                          pltpu.VMEM((n_sub, 2, 2 * tq), F32), pltpu.VMEM((1, hw), F32)]
    return pl.pallas_call(
        functools.partial(_diff_attn_kernel, lam_init=lam_init, tk=tk, tq=tq),
        out_shape=jax.ShapeDtypeStruct((bq, sq, d), BF16),
        grid=(bq, DIFF_HEADS, sq // tq_step),
        in_specs=in_specs,
        out_specs=pl.BlockSpec((1, tq_step, hw), lambda b, h, i: (b, i, h)),
        scratch_shapes=scratch,
        compiler_params=_cparams(3), name="diff_attn",
    )(*args)


def _nat_bias_rows():
    table = []
    for v, shift in enumerate((0, 4, 8)):
        rows = []
        for i in range(NAT_BAND):
            per_d = []
            for dq in range(NAT_QROWS):
                first = (0, dq, 4)[v]
                ok = first <= i < first + NAT_WIN_ROWS
                per_d.append(i - shift - dq + NAT_WIN_ROWS - 1 if ok else None)
            rows.append(per_d)
        table.append(rows)
    return table


def _nat_toeplitz_kernel(r_ref, o_ref):
    m, n = o_ref.shape

    def window(shape):
        col = lax.broadcasted_iota(jnp.int32, shape, 1)
        kc = lax.shift_right_logical(col, 7)
        qc = col & (GRID_W - 1)
        cs = jnp.clip(qc - NAT_WIN_COLS // 2, 0, GRID_W - NAT_WIN_COLS)
        return jnp.where(kc >= cs, jnp.where(kc < cs + NAT_WIN_COLS, kc - qc + NAT_WIN_COLS - 1, -1), -1)

    j = lax.broadcasted_iota(jnp.int32, (LANE, n), 0)
    onehot = jnp.where(j == window((LANE, n)), 1.0, 0.0).astype(BF16)
    r = r_ref[...]
    r_hi = r.astype(BF16)
    r1 = r - r_hi.astype(F32)
    r_mid = r1.astype(BF16)
    r_lo = (r1 - r_mid.astype(F32)).astype(BF16)
    t = (jnp.dot(r_hi, onehot, preferred_element_type=F32) + jnp.dot(r_mid, onehot, preferred_element_type=F32)
         + jnp.dot(r_lo, onehot, preferred_element_type=F32))
    o_ref[...] = jnp.where(window((m, n)) >= 0, t * LOG2_E, NEG)


def _nat_bias_kernel(t_ref, o_ref):
    lane = lax.broadcasted_iota(jnp.int32, (GRID_W, LANE), 1)
    neg = jnp.full((GRID_W, LANE), NEG, F32)
    rows = _nat_bias_rows()
    for v in range(3):
        for i in range(NAT_BAND):
            for pair in range(NAT_QROWS // 2):
                a_l, a_r = rows[v][i][2 * pair], rows[v][i][2 * pair + 1]
                left = neg if a_l is None else t_ref[0, a_l]
                right = neg if a_r is None else t_ref[0, a_r]
                o_ref[0, v, i * GRID_W:(i + 1) * GRID_W, pair * LANE:(pair + 1) * LANE] = (
                    jnp.where(lane < GRID_W, left, right))


def _nat_bias(rpb):
    nh, nr, nc = rpb.shape
    r2 = jnp.pad(rpb, ((0, 0), (0, 16 - nr), (0, LANE - nc))).reshape(nh * 16, LANE)
    n = GRID_W * LANE
    t = pl.pallas_call(
        _nat_toeplitz_kernel,
        out_shape=jax.ShapeDtypeStruct((nh * 16, n), F32),
        grid=(1,),
        in_specs=[pl.BlockSpec((nh * 16, LANE), lambda i: (0, 0))],
        out_specs=pl.BlockSpec((nh * 16, n), lambda i: (0, 0)),
        compiler_params=_cparams(1), name="nat_toeplitz",
    )(r2)
    t = t.reshape(nh, 16, GRID_W, LANE)
    nk, nq = NAT_BAND * GRID_W, NAT_QROWS * GRID_W
    return pl.pallas_call(
        _nat_bias_kernel,
        out_shape=jax.ShapeDtypeStruct((nh, 3, nk, nq), F32),
        grid=(nh,),
        in_specs=[pl.BlockSpec((1, 16, GRID_W, LANE), lambda h: (h, 0, 0, 0))],
        out_specs=pl.BlockSpec((1, 3, nk, nq), lambda h: (h, 0, 0, 0)),
        compiler_params=_cparams(1), name="nat_bias",
    )(t)


def _nat_attn_kernel(q_ref, kc_ref, vct_ref, *rest, n_rows, n_blk, group):
    if len(rest) == 6:
        bias_ref, k_ref, vt_ref, o_ref, sa_ref, sb_ref = rest
    else:
        bias_ref = k_ref = vt_ref = None
        o_ref, sa_ref, sb_ref = rest
    tq = NAT_QROWS * GRID_W
    nk = NAT_BAND * GRID_W
    n_ctx = kc_ref.shape[1]
    n_heads = LANE // NAT_HD
    dn = (((1,), (1,)), ((), ()))
    lane = lax.broadcasted_iota(jnp.int32, (tq, LANE), 1)
    zero = jnp.zeros((tq, LANE), BF16)

    def band(g):
        blk = pl.program_id(2) * group + g
        start = jnp.clip(blk * NAT_QROWS - NAT_WIN_ROWS // 2, 0, n_rows - NAT_BAND)
        return (pl.multiple_of(start * GRID_W, NAT_QROWS * GRID_W),
                jnp.where(blk == 0, 0, jnp.where(blk == n_blk - 1, 2, 1)))

    def produce(g, hh, s_ref):
        q = q_ref[0, g * tq:(g + 1) * tq, :]
        qh = jnp.where((lane >= hh * NAT_HD) & (lane < (hh + 1) * NAT_HD), q, zero)
        s_c = lax.dot_general(kc_ref[0], qh, dn, preferred_element_type=F32)
        s_ref[0:n_ctx, :] = s_c
        m = jnp.max(s_c, axis=0, keepdims=True)
        if k_ref is not None:
            off, variant = band(g)
            s_b = lax.dot_general(k_ref[0, pl.ds(off, nk), :], qh, dn, preferred_element_type=F32)
            s_b = s_b + bias_ref[hh, variant]
            s_ref[n_ctx:, :] = s_b
            m = jnp.maximum(m, jnp.max(s_b, axis=0, keepdims=True))
        return m

    def consume(g, hh, s_ref, m):
        rows = slice(hh * NAT_HD, (hh + 1) * NAT_HD)
        p = jnp.exp2(s_ref[...] - m)
        l = jnp.sum(p, axis=0, keepdims=True)
        pb = p.astype(BF16)
        o = jnp.dot(vct_ref[0, rows, :], pb[0:n_ctx], preferred_element_type=F32)
        if k_ref is not None:
            off, _ = band(g)
            o = o + jnp.dot(vt_ref[0, rows, pl.ds(off, nk)], pb[n_ctx:], preferred_element_type=F32)
        return o / l

    chains = [(g, hh) for g in range(group) for hh in range(n_heads)]
    bufs = (sa_ref, sb_ref)
    m = produce(*chains[0], bufs[0])
    outs = []
    for i, (g, hh) in enumerate(chains):
        if i + 1 < len(chains):
            m_next = produce(*chains[i + 1], bufs[(i + 1) % 2])
        outs.append(consume(g, hh, bufs[i % 2], m))
        m = m_next
        if hh == n_heads - 1:
            o_ref[0, g * tq:(g + 1) * tq, :] = jnp.concatenate(outs, axis=0).T.astype(BF16)
            outs = []


def _nat_attn(q, kc, vct, bias=None, k=None, vt=None):
    bq, sq, d = q.shape
    n_ctx = kc.shape[1]
    tq = NAT_QROWS * GRID_W
    n_blk = sq // tq
    group = math.gcd(n_blk, NAT_BLOCKS_PER_STEP)
    in_specs = [pl.BlockSpec((1, group * tq, LANE), lambda hp, b, i: (b, i, hp)),
                pl.BlockSpec((1, n_ctx, LANE), lambda hp, b, i: (b, 0, hp)),
                pl.BlockSpec((1, LANE, n_ctx), lambda hp, b, i: (b, hp, 0))]
    args = [q, kc, vct]
    n_rows = None
    n_keys = n_ctx
    if k is not None:
        s = k.shape[1]
        n_rows = s // GRID_W
        nk = NAT_BAND * GRID_W
        n_keys += nk
        in_specs += [pl.BlockSpec((LANE // NAT_HD, 3, nk, tq), lambda hp, b, i: (hp, 0, 0, 0)),
                     pl.BlockSpec((1, s, LANE), lambda hp, b, i: (b, 0, hp)),
                     pl.BlockSpec((1, LANE, s), lambda hp, b, i: (b, hp, 0))]
        args += [bias, k, vt]
    return pl.pallas_call(
        functools.partial(_nat_attn_kernel, n_rows=n_rows, n_blk=n_blk, group=group),
        out_shape=jax.ShapeDtypeStruct((bq, sq, d), BF16),
        grid=(d // LANE, bq, n_blk // group),
        in_specs=in_specs,
        out_specs=pl.BlockSpec((1, group * tq, LANE), lambda hp, b, i: (b, i, hp)),
        scratch_shapes=[pltpu.VMEM((n_keys, tq), F32)] * 2,
        compiler_params=_cparams(3), name="nat_attn",
    )(*args)


def kernel(x, c, ctx, c_ctx, ada_w, ada_b, norm_g, ffn_w_in, ffn_w_out, pool_w, pool_scale,
           diff_w_qkv, diff_lam, diff_subln_g, diff_w_o, nat_w_qkv, nat_b_qkv, nat_rpb,
           nat_w_o, nat_b_o, final_g):
    b, seq, d = x.shape
    mods = _ada_mods(c, c_ctx, ada_w, ada_b)
    w_in = ffn_w_in.astype(BF16)
    w_out = ffn_w_out.astype(BF16)
    xc = ctx
    for i in range(DEPTH):
        kind = i % N_MIXERS
        j = i // N_MIXERS
        last = i == DEPTH - 1
        update_ctx = not last
        ctx_needed = update_ctx or kind != 0
        mx = mods[i, :b].reshape(b, N_MOD, d)
        mc = mods[i, b:b + 1].reshape(1, N_MOD, d)

        x = _ffn(x, mx, norm_g[i, 0], w_in, w_out, (i, 0))
        if ctx_needed:
            xc = _ffn(xc, mc, norm_g[i, 0], w_in, w_out, (i, 0))

        if kind == 0:
            tail_x = tail_c = dict(pool=(norm_g[i, 1], pool_w[j].astype(BF16), pool_scale[j]))
        elif kind == 1:
            lam_init = 0.8 - 0.6 * math.exp(-0.3 * i)
            scale = DIFF_HD ** -0.5 * LOG2_E
            q_l, k_l, vt_l = _qkv(x, mx, norm_g[i, 1], diff_w_qkv[j], scale, rope_tabs=_rope_tables(seq))
            q_c, k_c, vt_c = _qkv(xc, mc, norm_g[i, 1], diff_w_qkv[j], scale)
            o_l = _diff_attn(diff_lam[j], diff_subln_g[j], q_l, k_c, vt_c, k_l, vt_l, lam_init)
            tail_x = dict(proj=(o_l, diff_w_o[j], None))
            if update_ctx:
                o_c = _diff_attn(diff_lam[j], diff_subln_g[j], q_c, k_c, vt_c, None, None, lam_init)
                tail_c = dict(proj=(o_c, diff_w_o[j], None))
        else:
            scale = NAT_HD ** -0.5 * LOG2_E
            q_l, k_l, vt_l = _qkv(x, mx, norm_g[i, 1], nat_w_qkv[j], scale, b_qkv=nat_b_qkv[j])
            q_c, k_c, vt_c = _qkv(xc, mc, norm_g[i, 1], nat_w_qkv[j], scale, b_qkv=nat_b_qkv[j])
            o_l = _nat_attn(q_l, k_c, vt_c, _nat_bias(nat_rpb[j]), k_l, vt_l)
            tail_x = dict(proj=(o_l, nat_w_o[j], nat_b_o[j]))
            if update_ctx:
                o_c = _nat_attn(q_c, k_c, vt_c)
                tail_c = dict(proj=(o_c, nat_w_o[j], nat_b_o[j]))

        x = _ffn(x, mx, norm_g[i, 2], w_in, w_out, (i, 1), final_g=final_g if last else None, **tail_x)
        if update_ctx:
            xc = _ffn(xc, mc, norm_g[i, 2], w_in, w_out, (i, 1), **tail_c)
    return x
```

```python
import functools
import math

import jax
import jax.numpy as jnp
from jax import lax
from jax.experimental import pallas as pl
from jax.experimental.pallas import tpu as pltpu

D_MODEL = 1024
DEPTH = 4
GRID_W = 64
N_MIXERS = 3
D_FF = 2816
N_MOD = 9
POOL_WINDOWS = (2, 4, 8, 16)
POOL_GW = 256
POOL_HALO = 8
POOL_SUBTILE = 256

TOKEN_TILE = 512
WIDE_TILE = 1024
DIFF_TQ = 256
DIFF_TK = 1024
DIFF_TILES_PER_STEP = 4
NAT_BLOCKS_PER_STEP = 16
DIFF_HEADS = 8
DIFF_HD = 64
NAT_HEADS = 16
NAT_HD = 64
NAT_WIN_ROWS = 8
NAT_WIN_COLS = 16
NAT_QROWS = 4
NAT_BAND = 12
ROPE_THETA = 10000.0
NORM_EPS = 1e-6
NEG = -1e30
LOG2_E = math.log2(math.e)
BOUND_MARGIN = 1.01
FIXED_OFFSET_SPAN = 64.0

LANE = 128
FF_CHUNK = 256
VMEM_LIMIT = 56 * 1024 * 1024

F32 = jnp.float32
BF16 = jnp.bfloat16


def _cparams(n_axes):
    return pltpu.CompilerParams(dimension_semantics=("arbitrary",) * n_axes, vmem_limit_bytes=VMEM_LIMIT)


def _const_spec(shape):
    nd = len(shape)
    return pl.BlockSpec(shape, lambda *_: (0,) * nd, pipeline_mode=pl.Buffered(1))


def _mod_spec(mod):
    if mod.shape[0] == 1:
        return pl.BlockSpec((1, N_MOD, D_MODEL), lambda b, i: (0, 0, 0))
    return pl.BlockSpec((1, N_MOD, D_MODEL), lambda b, i: (b, 0, 0))


def _sigmoid(x):
    return 1.0 / (1.0 + jnp.exp(-x))


def _rms(x):
    return x * lax.rsqrt(jnp.mean(x * x, axis=-1, keepdims=True) + NORM_EPS)


def _norm_mod(x, g, mod_ref, row):
    shift = mod_ref[0, row:row + 1, :]
    scale = mod_ref[0, row + 1:row + 2, :]
    return (_rms(x) * g) * (1.0 + scale) + shift


def _ada_kernel(c_ref, w_ref, b_ref, o_ref):
    c = c_ref[...]
    s = (c * _sigmoid(c)).astype(BF16)
    o_ref[0] = jnp.dot(s, w_ref[0].astype(BF16), preferred_element_type=F32) + b_ref[0]


def _ada_mods(c, c_ctx, ada_w, ada_b):
    d = D_MODEL
    cc = jnp.concatenate([c, c_ctx[None], jnp.zeros((3, d), F32)], axis=0)
    tn = WIDE_TILE
    return pl.pallas_call(
        _ada_kernel,
        out_shape=jax.ShapeDtypeStruct((DEPTH, 8, N_MOD * d), F32),
        grid=(DEPTH, N_MOD * d // tn),
        in_specs=[pl.BlockSpec((8, d), lambda l, n: (0, 0)),
                  pl.BlockSpec((1, d, tn), lambda l, n: (l, 0, n)),
                  pl.BlockSpec((1, 1, tn), lambda l, n: (l, 0, n))],
        out_specs=pl.BlockSpec((1, 8, tn), lambda l, n: (l, 0, n)),
        compiler_params=_cparams(2), name="ada_mods",
    )(cc, ada_w, ada_b.reshape(DEPTH, 1, N_MOD * d))


def _swiglu_half_step(x, mod_ref, g_ref, win_ref, wout_ref, a_ref, row, r0=0, side_work=()):
    rows = slice(r0, r0 + x.shape[0])
    side_work = list(side_work)
    h = _norm_mod(x, g_ref[...], mod_ref, row).astype(BF16)
    for j in range(D_FF // FF_CHUNK):
        lo = j * FF_CHUNK
        g = jnp.dot(h, win_ref[0, 0, :, lo:lo + FF_CHUNK], preferred_element_type=F32)
        u = jnp.dot(h, win_ref[0, 0, :, D_FF + lo:D_FF + lo + FF_CHUNK], preferred_element_type=F32)
        a_ref[rows, lo:lo + FF_CHUNK] = ((g * _sigmoid(g)) * u).astype(BF16)
        if side_work and j % 2 == 1:
            side_work.pop(0)()
    for thunk in side_work:
        thunk()
    y = jnp.dot(a_ref[rows, :], wout_ref[0, 0], preferred_element_type=F32)
    return x + (0.5 * mod_ref[0, row + 2:row + 3, :]) * y


def _pool_fill(x_ref, xp_ref, xn_ref, mod_ref, g_ref, ext_ref):
    i = pl.program_id(1)
    last = pl.num_programs(1) - 1
    tm = x_ref.shape[1]
    g = g_ref[...]
    ext_ref[0:POOL_HALO, :] = jnp.where(i > 0, _norm_mod(xp_ref[0], g, mod_ref, 3), 0.0)
    ext_ref[POOL_HALO:POOL_HALO + tm, :] = _norm_mod(x_ref[0], g, mod_ref, 3)
    ext_ref[POOL_HALO + tm:, :] = jnp.where(i < last, _norm_mod(xn_ref[0], g, mod_ref, 3), 0.0)


def _pool_group(x_ref, mod_ref, pw_ref, ps_ref, ext_ref, seq, r0, n, gi):
    tm = x_ref.shape[1]
    win = POOL_WINDOWS[gi]
    t = pl.program_id(1) * tm + r0 + lax.broadcasted_iota(jnp.int32, (n, 1), 0)
    base = POOL_HALO + r0
    cols = slice(gi * POOL_GW, (gi + 1) * POOL_GW)
    lo_off = -(win // 2)
    acc = ext_ref[base + lo_off:base + lo_off + n, cols]
    for o in range(lo_off + 1, lo_off + win):
        acc = acc + ext_ref[base + o:base + o + n, cols]
    lo = jnp.clip(t + lo_off, 0, seq - 1)
    hi = jnp.clip(t + lo_off + win - 1, 0, seq - 1)
    cnt = (hi - lo + 1).astype(F32)
    diff = (acc / cnt - ext_ref[base:base + n, cols]).astype(BF16)
    y = jnp.dot(diff, pw_ref[gi], preferred_element_type=F32) * ps_ref[:, cols]
    return x_ref[0, r0:r0 + n, cols] + mod_ref[0, 5:6, cols] * y


def _ffn_kernel(*refs, mixer, seq, bias, final):
    refs = list(refs)
    x_ref = refs.pop(0)
    if mixer == "pool":
        xp_ref, xn_ref = refs.pop(0), refs.pop(0)
    elif mixer == "proj":
        attn_ref = refs.pop(0)
    mod_ref = refs.pop(0)
    if mixer == "pool":
        g1_ref, pw_ref, ps_ref = refs.pop(0), refs.pop(0), refs.pop(0)
    elif mixer == "proj":
        wo_ref = refs.pop(0)
        bo_ref = refs.pop(0) if bias else None
    g_ref, win_ref, wout_ref = refs.pop(0), refs.pop(0), refs.pop(0)
    fg_ref = refs.pop(0) if final else None
    o_ref, a_ref = refs.pop(0), refs.pop(0)
    tm = x_ref.shape[1]

    def finish(out):
        return _rms(out) * fg_ref[...] if final else out

    if mixer == "pool":
        ext_ref = refs.pop(0)
        n = min(tm, POOL_SUBTILE)
        n_groups = len(POOL_WINDOWS)
        _pool_fill(x_ref, xp_ref, xn_ref, mod_ref, g1_ref, ext_ref)
        group = functools.partial(_pool_group, x_ref, mod_ref, pw_ref, ps_ref, ext_ref, seq)
        cur = [group(0, n, gi) for gi in range(n_groups)]
        for r0 in range(0, tm, n):
            nxt, side = [], []
            if r0 + n < tm:
                side = [lambda gi=gi, r=r0 + n: nxt.append(group(r, n, gi)) for gi in range(n_groups)]
            out = _swiglu_half_step(jnp.concatenate(cur, axis=1), mod_ref, g_ref, win_ref, wout_ref, a_ref, 6, r0, side)
            o_ref[0, r0:r0 + n, :] = finish(out)
            cur = nxt
        return
    if mixer == "proj":
        y = jnp.dot(attn_ref[0], wo_ref[...], preferred_element_type=F32)
        if bias:
            y = y + bo_ref[...]
        x = x_ref[0] + mod_ref[0, 5:6, :] * y
        row = 6
    else:
        x = x_ref[0]
        row = 0
    o_ref[0] = finish(_swiglu_half_step(x, mod_ref, g_ref, win_ref, wout_ref, a_ref, row))


def _ffn(x, mod, g, w_in, w_out, widx, pool=None, proj=None, final_g=None):
    bx, sx, d = x.shape
    tm = min(TOKEN_TILE, sx)
    tok = pl.BlockSpec((1, tm, d), lambda b, i: (b, i, 0))
    in_specs, args, scratch = [tok], [x], [pltpu.VMEM((tm, D_FF), BF16)]
    mixer = None
    if pool is not None:
        mixer = "pool"
        nb, n8 = tm // POOL_HALO, sx // POOL_HALO
        in_specs += [pl.BlockSpec((1, POOL_HALO, d), lambda b, i: (b, jnp.maximum(i * nb - 1, 0), 0)),
                     pl.BlockSpec((1, POOL_HALO, d), lambda b, i: (b, jnp.minimum((i + 1) * nb, n8 - 1), 0))]
        args += [x, x]
        scratch.append(pltpu.VMEM((tm + 2 * POOL_HALO, d), F32))
    elif proj is not None:
        mixer = "proj"
        in_specs.append(tok)
        args.append(proj[0])
    in_specs.append(_mod_spec(mod))
    args.append(mod)
    if pool is not None:
        in_specs += [_const_spec((1, d)), _const_spec((len(POOL_WINDOWS), POOL_GW, POOL_GW)), _const_spec((1, d))]
        args += [pool[0].reshape(1, d), pool[1], pool[2].reshape(1, d)]
    elif proj is not None:
        in_specs.append(_const_spec((d, d)))
        args.append(proj[1].astype(BF16))
        if proj[2] is not None:
            in_specs.append(_const_spec((1, d)))
            args.append(proj[2].reshape(1, d))
    in_specs += [_const_spec((1, d)),
                 pl.BlockSpec((1, 1, d, 2 * D_FF), lambda b, i: widx + (0, 0), pipeline_mode=pl.Buffered(1)),
                 pl.BlockSpec((1, 1, D_FF, d), lambda b, i: widx + (0, 0), pipeline_mode=pl.Buffered(1))]
    args += [g.reshape(1, d), w_in, w_out]
    if final_g is not None:
        in_specs.append(_const_spec((1, d)))
        args.append(final_g.reshape(1, d))
    return pl.pallas_call(
        functools.partial(_ffn_kernel, mixer=mixer, seq=sx, bias=proj is not None and proj[2] is not None,
                          final=final_g is not None),
        out_shape=jax.ShapeDtypeStruct(x.shape, F32),
        grid=(bx, sx // tm),
        in_specs=in_specs,
        out_specs=tok,
        scratch_shapes=scratch,
        compiler_params=_cparams(2), name="ffn" if mixer is None else mixer + "_ffn",
    )(*args)


def _rope_table_kernel(inv_ref, cos_ref, sin_ref):
    tm = cos_ref.shape[0]
    t = pl.program_id(0) * tm + lax.broadcasted_iota(jnp.int32, (tm, LANE), 0)
    lane = lax.broadcasted_iota(jnp.int32, (tm, LANE), 1)
    pos = jnp.where((lane & 63) < 32, lax.shift_right_logical(t, 6), t & (GRID_W - 1)).astype(F32)
    ang = pos * inv_ref[...]
    cos_ref[...] = jnp.cos(ang)
    sin_ref[...] = jnp.where((lane & 31) < 16, -jnp.sin(ang), jnp.sin(ang))


def _rope_tables(seq):
    per_axis = DIFF_HD // 2
    inv = ROPE_THETA ** (-jnp.arange(0, per_axis, 2, dtype=F32) / per_axis)
    inv_lane = jnp.tile(inv, LANE // inv.shape[0]).reshape(1, LANE)
    tm = min(WIDE_TILE, seq)
    return pl.pallas_call(
        _rope_table_kernel,
        out_shape=(jax.ShapeDtypeStruct((seq, LANE), F32),) * 2,
        grid=(seq // tm,),
        in_specs=[pl.BlockSpec((1, LANE), lambda i: (0, 0))],
        out_specs=(pl.BlockSpec((tm, LANE), lambda i: (i, 0)),) * 2,
        compiler_params=_cparams(1), name="rope_tables",
    )(inv_lane)


def _qkv_kernel(x_ref, mod_ref, g_ref, wqk_ref, wvt_ref, *rest, rope, bias, scale):
    rest = list(rest)
    if bias:
        bqk_ref, bvt_ref = rest[:2]
        rest = rest[2:]
    if rope:
        cos_ref, sin_ref = rest[:2]
        rest = rest[2:]
    q_ref, k_ref, vt_ref = rest
    d = x_ref.shape[2]
    h = _norm_mod(x_ref[0], g_ref[...], mod_ref, 3).astype(BF16)
    vt = lax.dot_general(wvt_ref[...], h, (((1,), (1,)), ((), ())), preferred_element_type=F32)
    if bias:
        vt = vt + bvt_ref[...]
    vt_ref[0] = vt.astype(BF16)
    if rope:
        cos = cos_ref[...]
        sin = sin_ref[...]
        lane = lax.broadcasted_iota(jnp.int32, cos.shape, 1)
        first_half = (lane & 31) < 16
    for cb in range(2 * d // FF_CHUNK):
        c0 = cb * FF_CHUNK
        t2 = jnp.dot(h, wqk_ref[:, c0:c0 + FF_CHUNK], preferred_element_type=F32)
        if bias:
            t2 = t2 + bqk_ref[:, c0:c0 + FF_CHUNK]
        for half in range(FF_CHUNK // LANE):
            t = t2[:, half * LANE:(half + 1) * LANE]
            if rope:
                partner = jnp.where(first_half, pltpu.roll(t, LANE - 16, 1), pltpu.roll(t, 16, 1))
                t = t * cos + partner * sin
            col = c0 + half * LANE
            if col < d:
                q_ref[0, :, col:col + LANE] = (t * scale).astype(BF16)
            else:
                k_ref[0, :, col - d:col - d + LANE] = t.astype(BF16)


def _qkv(x, mod, g, w_qkv, scale, b_qkv=None, rope_tabs=None):
    bx, sx, d = x.shape
    tm = min(TOKEN_TILE, sx)
    wqk = w_qkv[:, :2 * d].astype(BF16)
    wvt = w_qkv[:, 2 * d:].T.astype(BF16)
    in_specs = [pl.BlockSpec((1, tm, d), lambda b, i: (b, i, 0)), _mod_spec(mod), _const_spec((1, d)),
                _const_spec((d, 2 * d)), _const_spec((d, d))]
    args = [x, mod, g.reshape(1, d), wqk, wvt]
    if b_qkv is not None:
        in_specs += [_const_spec((1, 2 * d)), _const_spec((d, 1))]
        args += [b_qkv[:2 * d].reshape(1, 2 * d), b_qkv[2 * d:].reshape(d, 1)]
    if rope_tabs is not None:
        in_specs += [pl.BlockSpec((tm, LANE), lambda b, i: (i, 0))] * 2
        args += list(rope_tabs)
    tok = pl.BlockSpec((1, tm, d), lambda b, i: (b, i, 0))
    return pl.pallas_call(
        functools.partial(_qkv_kernel, rope=rope_tabs is not None, bias=b_qkv is not None, scale=scale),
        out_shape=(jax.ShapeDtypeStruct((bx, sx, d), BF16), jax.ShapeDtypeStruct((bx, sx, d), BF16),
                   jax.ShapeDtypeStruct((bx, d, sx), BF16)),
        grid=(bx, sx // tm),
        in_specs=in_specs,
        out_specs=(tok, tok, pl.BlockSpec((1, d, tm), lambda b, i: (b, 0, i))),
        compiler_params=_cparams(2), name="qkv_proj",
    )(*args)


def _subhead_sq_norms(x, ones):
    xf = x.astype(F32)
    return jnp.dot((xf * xf).astype(BF16), ones, preferred_element_type=F32)


def _diff_attn_kernel(lam_ref, g_ref, q_ref, kc_ref, vct_ref, *rest, lam_init, tk, tq):
    if len(rest) == 10:
        k_ref, vt_ref, o_ref, acc_ref, sa_ref, sb_ref, sc_ref, ml_ref, kn_ref, acc2_ref = rest
    else:
        k_ref = vt_ref = None
        o_ref, acc_ref = rest
        sa_ref = sb_ref = sc_ref = ml_ref = kn_ref = acc2_ref = None
    n_sub = q_ref.shape[1] // tq
    hw = 2 * DIFF_HD
    n_chunks = None

    if k_ref is not None:
        n_chunks = k_ref.shape[1] // tk

        @pl.when(pl.program_id(2) == 0)
        def _():
            r = lax.broadcasted_iota(jnp.int32, (hw, hw), 0)
            c_ = lax.broadcasted_iota(jnp.int32, (hw, hw), 1)
            same_subhead = jnp.where((r < DIFF_HD) == (c_ < DIFF_HD), 1.0, 0.0).astype(BF16)
            kn = jnp.max(_subhead_sq_norms(kc_ref[0], same_subhead), axis=0, keepdims=True)
            for c in range(n_chunks):
                kn = jnp.maximum(kn, jnp.max(_subhead_sq_norms(k_ref[0, c * tk:(c + 1) * tk, :], same_subhead),
                                             axis=0, keepdims=True))
            kn_ref[...] = kn

    def split_subheads(rows):
        q = q_ref[0, rows, :]
        lane = lax.broadcasted_iota(jnp.int32, q.shape, 1)
        zero = jnp.zeros_like(q)
        return q, jnp.concatenate([jnp.where(lane < DIFF_HD, q, zero), jnp.where(lane >= DIFF_HD, q, zero)], axis=0)

    def scores(k_t, qq):
        return lax.dot_general(k_t, qq, (((1,), (1,)), ((), ())), preferred_element_type=F32)

    def context_block(s, m, acc_ref=acc_ref):
        p = jnp.exp2(s - m)
        acc_ref[...] = jnp.dot(vct_ref[0], p.astype(BF16), preferred_element_type=F32)
        return jnp.sum(p, axis=0, keepdims=True)

    def finish(rows, l, acc_ref=acc_ref):
        lf = lam_ref[...]
        lam = (jnp.exp(jnp.sum(lf[0:1] * lf[1:2], axis=-1, keepdims=True))
               - jnp.exp(jnp.sum(lf[2:3] * lf[3:4], axis=-1, keepdims=True)) + lam_init)
        acc = acc_ref[...]
        inv = 1.0 / l
        o = acc[:, :tq] * inv[:, :tq] - lam * (acc[:, tq:] * inv[:, tq:])
        o = o * lax.rsqrt(jnp.mean(o * o, axis=0, keepdims=True) + NORM_EPS)
        o = (o * g_ref[...]) * (1.0 - lam_init)
        o_ref[0, rows, :] = o.T.astype(BF16)

    if k_ref is None:
        assert n_sub == 1
        rows = slice(0, tq)
        _, qq = split_subheads(rows)
        s = scores(kc_ref[0], qq)
        finish(rows, context_block(s, jnp.max(s, axis=0, keepdims=True)))
        return

    r = lax.broadcasted_iota(jnp.int32, (16, hw), 0)
    c_ = lax.broadcasted_iota(jnp.int32, (16, hw), 1)
    select = jnp.where(r == jnp.where(c_ < DIFF_HD, 0, 1), 1.0, 0.0).astype(BF16)
    kn = kn_ref[...]
    gap = None
    for t in range(n_sub):
        q, qq = split_subheads(slice(t * tq, (t + 1) * tq))
        s = scores(kc_ref[0], qq)
        m = jnp.max(s, axis=0, keepdims=True)
        qf = q.astype(F32)
        qn = lax.dot_general(select, (qf * qf).astype(BF16), (((1,), (1,)), ((), ())),
                             preferred_element_type=F32)
        bound = jnp.concatenate([jnp.sqrt(qn[0:1, :] * kn[:, 0:1]),
                                 jnp.sqrt(qn[1:2, :] * kn[:, DIFF_HD:DIFF_HD + 1])], axis=1) * BOUND_MARGIN
        sc_ref[t] = s
        ml_ref[t, 0:1, :] = m
        ml_ref[t, 1:2, :] = bound
        gap = bound - m if gap is None else jnp.maximum(gap, bound - m)
    fixed = jnp.max(gap) <= FIXED_OFFSET_SPAN

    def run_tiles(online):
        def produce(c, s_ref, qq):
            s = scores(k_ref[0, c * tk:(c + 1) * tk, :], qq)
            s_ref[...] = s
            return jnp.max(s, axis=0, keepdims=True) if online else None

        def consume(s_ref, c, s_max, m, l):
            if online:
                m_new = jnp.maximum(m, s_max)
                alpha = jnp.exp2(m - m_new)
                p = jnp.exp2(s_ref[...] - m_new)
                l = alpha * l + jnp.sum(p, axis=0, keepdims=True)
                pv_prev = alpha * acc_ref[...]
            else:
                p = jnp.exp2(s_ref[...] - m)
                l = l + jnp.sum(p, axis=0, keepdims=True)
                pv_prev = acc_ref[...]
                m_new = m
            vt_t = vt_ref[0, :, c * tk:(c + 1) * tk]
            acc_ref[...] = pv_prev + jnp.dot(vt_t, p.astype(BF16), preferred_element_type=F32)
            return m_new, l

        def fixed_offset_tile(t, acc):
            rows = pl.ds(pl.multiple_of(t * tq, tq), tq)
            _, qq = split_subheads(rows)
            m = ml_ref[t, 1:2, :]
            l = context_block(sc_ref[t], m, acc)
            for c in range(n_chunks):
                p = jnp.exp2(scores(k_ref[0, c * tk:(c + 1) * tk, :], qq) - m)
                l = l + jnp.sum(p, axis=0, keepdims=True)
                acc[...] += jnp.dot(vt_ref[0, :, c * tk:(c + 1) * tk], p.astype(BF16), preferred_element_type=F32)
            finish(rows, l, acc)

        def tile_pair(t, carry):
            fixed_offset_tile(2 * t, acc_ref)
            fixed_offset_tile(2 * t + 1, acc2_ref)
            return carry

        if not online:
            if n_sub % 2 == 0:
                lax.fori_loop(0, n_sub // 2, tile_pair, 0)
            else:
                lax.fori_loop(0, n_sub, lambda t, carry: (fixed_offset_tile(t, acc_ref), carry)[1], 0)
            return

        def tile(t, carry):
            rows = pl.ds(pl.multiple_of(t * tq, tq), tq)
            _, qq = split_subheads(rows)
            m = ml_ref[t, 0:1, :]
            bufs = (sa_ref, sb_ref)
            s_max = produce(0, bufs[0], qq)
            l = context_block(sc_ref[t], m)
            for c in range(n_chunks):
                if c + 1 < n_chunks:
                    next_max = produce(c + 1, bufs[(c + 1) % 2], qq)
                m, l = consume(bufs[c % 2], c, s_max, m, l)
                s_max = next_max
            finish(rows, l)
            return carry

        lax.fori_loop(0, n_sub, tile, 0)

    pl.when(fixed)(functools.partial(run_tiles, False))
    pl.when(jnp.logical_not(fixed))(functools.partial(run_tiles, True))


def _diff_attn(lam, subln_g, q, kc, vct, k, vt, lam_init):
    bq, sq, d = q.shape
    n_ctx = kc.shape[1]
    tq = DIFF_TQ
    tq_step = tq * math.gcd(sq // tq, DIFF_TILES_PER_STEP)
    hw = 2 * DIFF_HD
    in_specs = [_const_spec((4, DIFF_HD)), _const_spec((hw, 1)),
                pl.BlockSpec((1, tq_step, hw), lambda b, h, i: (b, i, h)),
                pl.BlockSpec((1, n_ctx, hw), lambda b, h, i: (b, 0, h)),
                pl.BlockSpec((1, hw, n_ctx), lambda b, h, i: (b, h, 0))]
    args = [lam, subln_g.reshape(hw, 1), q, kc, vct]
    scratch = [pltpu.VMEM((hw, 2 * tq), F32)]
    tk = DIFF_TK
    if k is not None:
        s = k.shape[1]
        assert s % tk == 0
        in_specs += [pl.BlockSpec((1, s, hw), lambda b, h, i: (b, 0, h)),
                     pl.BlockSpec((1, hw, s), lambda b, h, i: (b, h, 0))]
        args += [k, vt]
        n_sub = tq_step // tq
        scratch += [pltpu.VMEM((tk, 2 * tq), F32)] * 2 + [pltpu.VMEM((n_sub, n_ctx, 2 * tq), F32),
                                                          pltpu.VMEM((n_sub, 2, 2 * tq), F32), pltpu.VMEM((1, hw), F32),
                                                          pltpu.VMEM((hw, 2 * tq), F32)]
    return pl.pallas_call(
        functools.partial(_diff_attn_kernel, lam_init=lam_init, tk=tk, tq=tq),
        out_shape=jax.ShapeDtypeStruct((bq, sq, d), BF16),
        grid=(bq, DIFF_HEADS, sq // tq_step),
        in_specs=in_specs,
        out_specs=pl.BlockSpec((1, tq_step, hw), lambda b, h, i: (b, i, h)),
        scratch_shapes=scratch,
        compiler_params=_cparams(3), name="diff_attn",
    )(*args)


def _nat_bias_rows():
    table = []
    for v, shift in enumerate((0, 4, 8)):
        rows = []
        for i in range(NAT_BAND):
            per_d = []
            for dq in range(NAT_QROWS):
                first = (0, dq, 4)[v]
                ok = first <= i < first + NAT_WIN_ROWS
                per_d.append(i - shift - dq + NAT_WIN_ROWS - 1 if ok else None)
            rows.append(per_d)
        table.append(rows)
    return table


def _nat_toeplitz_kernel(r_ref, o_ref):
    m, n = o_ref.shape

    def window(shape):
        col = lax.broadcasted_iota(jnp.int32, shape, 1)
        kc = lax.shift_right_logical(col, 7)
        qc = col & (GRID_W - 1)
        cs = jnp.clip(qc - NAT_WIN_COLS // 2, 0, GRID_W - NAT_WIN_COLS)
        return jnp.where(kc >= cs, jnp.where(kc < cs + NAT_WIN_COLS, kc - qc + NAT_WIN_COLS - 1, -1), -1)

    j = lax.broadcasted_iota(jnp.int32, (LANE, n), 0)
    onehot = jnp.where(j == window((LANE, n)), 1.0, 0.0).astype(BF16)
    r = r_ref[...]
    r_hi = r.astype(BF16)
    r1 = r - r_hi.astype(F32)
    r_mid = r1.astype(BF16)
    r_lo = (r1 - r_mid.astype(F32)).astype(BF16)
    t = (jnp.dot(r_hi, onehot, preferred_element_type=F32) + jnp.dot(r_mid, onehot, preferred_element_type=F32)
         + jnp.dot(r_lo, onehot, preferred_element_type=F32))
    o_ref[...] = jnp.where(window((m, n)) >= 0, t * LOG2_E, NEG)


def _nat_bias_kernel(t_ref, o_ref):
    lane = lax.broadcasted_iota(jnp.int32, (GRID_W, LANE), 1)
    neg = jnp.full((GRID_W, LANE), NEG, F32)
    rows = _nat_bias_rows()
    for v in range(3):
        for i in range(NAT_BAND):
            for pair in range(NAT_QROWS // 2):
                a_l, a_r = rows[v][i][2 * pair], rows[v][i][2 * pair + 1]
                left = neg if a_l is None else t_ref[0, a_l]
                right = neg if a_r is None else t_ref[0, a_r]
                o_ref[0, v, i * GRID_W:(i + 1) * GRID_W, pair * LANE:(pair + 1) * LANE] = (
                    jnp.where(lane < GRID_W, left, right))


def _nat_bias(rpb):
    nh, nr, nc = rpb.shape
    r2 = jnp.pad(rpb, ((0, 0), (0, 16 - nr), (0, LANE - nc))).reshape(nh * 16, LANE)
    n = GRID_W * LANE
    t = pl.pallas_call(
        _nat_toeplitz_kernel,
        out_shape=jax.ShapeDtypeStruct((nh * 16, n), F32),
        grid=(1,),
        in_specs=[pl.BlockSpec((nh * 16, LANE), lambda i: (0, 0))],
        out_specs=pl.BlockSpec((nh * 16, n), lambda i: (0, 0)),
        compiler_params=_cparams(1), name="nat_toeplitz",
    )(r2)
    t = t.reshape(nh, 16, GRID_W, LANE)
    nk, nq = NAT_BAND * GRID_W, NAT_QROWS * GRID_W
    return pl.pallas_call(
        _nat_bias_kernel,
        out_shape=jax.ShapeDtypeStruct((nh, 3, nk, nq), F32),
        grid=(nh,),
        in_specs=[pl.BlockSpec((1, 16, GRID_W, LANE), lambda h: (h, 0, 0, 0))],
        out_specs=pl.BlockSpec((1, 3, nk, nq), lambda h: (h, 0, 0, 0)),
        compiler_params=_cparams(1), name="nat_bias",
    )(t)


def _nat_attn_kernel(q_ref, kc_ref, vct_ref, *rest, n_rows, n_blk, group):
    if len(rest) == 6:
        bias_ref, k_ref, vt_ref, o_ref, sa_ref, sb_ref = rest
    else:
        bias_ref = k_ref = vt_ref = None
        o_ref, sa_ref, sb_ref = rest
    tq = NAT_QROWS * GRID_W
    nk = NAT_BAND * GRID_W
    n_ctx = kc_ref.shape[1]
    n_heads = LANE // NAT_HD
    dn = (((1,), (1,)), ((), ()))
    lane = lax.broadcasted_iota(jnp.int32, (tq, LANE), 1)
    zero = jnp.zeros((tq, LANE), BF16)

    def band(g):
        blk = pl.program_id(2) * group + g
        start = jnp.clip(blk * NAT_QROWS - NAT_WIN_ROWS // 2, 0, n_rows - NAT_BAND)
        return (pl.multiple_of(start * GRID_W, NAT_QROWS * GRID_W),
                jnp.where(blk == 0, 0, jnp.where(blk == n_blk - 1, 2, 1)))

    def produce(g, hh, s_ref):
        q = q_ref[0, g * tq:(g + 1) * tq, :]
        qh = jnp.where((lane >= hh * NAT_HD) & (lane < (hh + 1) * NAT_HD), q, zero)
        s_c = lax.dot_general(kc_ref[0], qh, dn, preferred_element_type=F32)
        s_ref[0:n_ctx, :] = s_c
        m = jnp.max(s_c, axis=0, keepdims=True)
        if k_ref is not None:
            off, variant = band(g)
            s_b = lax.dot_general(k_ref[0, pl.ds(off, nk), :], qh, dn, preferred_element_type=F32)
            s_b = s_b + bias_ref[hh, variant]
            s_ref[n_ctx:, :] = s_b
            m = jnp.maximum(m, jnp.max(s_b, axis=0, keepdims=True))
        return m

    def consume(g, hh, s_ref, m):
        rows = slice(hh * NAT_HD, (hh + 1) * NAT_HD)
        p = jnp.exp2(s_ref[...] - m)
        l = jnp.sum(p, axis=0, keepdims=True)
        pb = p.astype(BF16)
        o = jnp.dot(vct_ref[0, rows, :], pb[0:n_ctx], preferred_element_type=F32)
        if k_ref is not None:
            off, _ = band(g)
            o = o + jnp.dot(vt_ref[0, rows, pl.ds(off, nk)], pb[n_ctx:], preferred_element_type=F32)
        return o / l

    chains = [(g, hh) for g in range(group) for hh in range(n_heads)]
    bufs = (sa_ref, sb_ref)
    m = produce(*chains[0], bufs[0])
    outs = []
    for i, (g, hh) in enumerate(chains):
        if i + 1 < len(chains):
            m_next = produce(*chains[i + 1], bufs[(i + 1) % 2])
        outs.append(consume(g, hh, bufs[i % 2], m))
        m = m_next
        if hh == n_heads - 1:
            o_ref[0, g * tq:(g + 1) * tq, :] = jnp.concatenate(outs, axis=0).T.astype(BF16)
            outs = []


def _nat_attn(q, kc, vct, bias=None, k=None, vt=None):
    bq, sq, d = q.shape
    n_ctx = kc.shape[1]
    tq = NAT_QROWS * GRID_W
    n_blk = sq // tq
    group = math.gcd(n_blk, NAT_BLOCKS_PER_STEP)
    in_specs = [pl.BlockSpec((1, group * tq, LANE), lambda hp, b, i: (b, i, hp)),
                pl.BlockSpec((1, n_ctx, LANE), lambda hp, b, i: (b, 0, hp)),
                pl.BlockSpec((1, LANE, n_ctx), lambda hp, b, i: (b, hp, 0))]
    args = [q, kc, vct]
    n_rows = None
    n_keys = n_ctx
    if k is not None:
        s = k.shape[1]
        n_rows = s // GRID_W
        nk = NAT_BAND * GRID_W
        n_keys += nk
        in_specs += [pl.BlockSpec((LANE // NAT_HD, 3, nk, tq), lambda hp, b, i: (hp, 0, 0, 0)),
                     pl.BlockSpec((1, s, LANE), lambda hp, b, i: (b, 0, hp)),
                     pl.BlockSpec((1, LANE, s), lambda hp, b, i: (b, hp, 0))]
        args += [bias, k, vt]
    return pl.pallas_call(
        functools.partial(_nat_attn_kernel, n_rows=n_rows, n_blk=n_blk, group=group),
        out_shape=jax.ShapeDtypeStruct((bq, sq, d), BF16),
        grid=(d // LANE, bq, n_blk // group),
        in_specs=in_specs,
        out_specs=pl.BlockSpec((1, group * tq, LANE), lambda hp, b, i: (b, i, hp)),
        scratch_shapes=[pltpu.VMEM((n_keys, tq), F32)] * 2,
        compiler_params=_cparams(3), name="nat_attn",
    )(*args)


def kernel(x, c, ctx, c_ctx, ada_w, ada_b, norm_g, ffn_w_in, ffn_w_out, pool_w, pool_scale,
           diff_w_qkv, diff_lam, diff_subln_g, diff_w_o, nat_w_qkv, nat_b_qkv, nat_rpb,
           nat_w_o, nat_b_o, final_g):
    b, seq, d = x.shape
    mods = _ada_mods(c, c_ctx, ada_w, ada_b)
    w_in = ffn_w_in.astype(BF16)
    w_out = ffn_w_out.astype(BF16)
    xc = ctx
    for i in range(DEPTH):
        kind = i % N_MIXERS
        j = i // N_MIXERS
        last = i == DEPTH - 1
        update_ctx = not last
        ctx_needed = update_ctx or kind != 0
        mx = mods[i, :b].reshape(b, N_MOD, d)
        mc = mods[i, b:b + 1].reshape(1, N_MOD, d)

        x = _ffn(x, mx, norm_g[i, 0], w_in, w_out, (i, 0))
        if ctx_needed:
            xc = _ffn(xc, mc, norm_g[i, 0], w_in, w_out, (i, 0))

        if kind == 0:
            tail_x = tail_c = dict(pool=(norm_g[i, 1], pool_w[j].astype(BF16), pool_scale[j]))
        elif kind == 1:
            lam_init = 0.8 - 0.6 * math.exp(-0.3 * i)
            scale = DIFF_HD ** -0.5 * LOG2_E
            q_l, k_l, vt_l = _qkv(x, mx, norm_g[i, 1], diff_w_qkv[j], scale, rope_tabs=_rope_tables(seq))
            q_c, k_c, vt_c = _qkv(xc, mc, norm_g[i, 1], diff_w_qkv[j], scale)
            o_l = _diff_attn(diff_lam[j], diff_subln_g[j], q_l, k_c, vt_c, k_l, vt_l, lam_init)
            tail_x = dict(proj=(o_l, diff_w_o[j], None))
            if update_ctx:
                o_c = _diff_attn(diff_lam[j], diff_subln_g[j], q_c, k_c, vt_c, None, None, lam_init)
                tail_c = dict(proj=(o_c, diff_w_o[j], None))
        else:
            scale = NAT_HD ** -0.5 * LOG2_E
            q_l, k_l, vt_l = _qkv(x, mx, norm_g[i, 1], nat_w_qkv[j], scale, b_qkv=nat_b_qkv[j])
            q_c, k_c, vt_c = _qkv(xc, mc, norm_g[i, 1], nat_w_qkv[j], scale, b_qkv=nat_b_qkv[j])
            o_l = _nat_attn(q_l, k_c, vt_c, _nat_bias(nat_rpb[j]), k_l, vt_l)
            tail_x = dict(proj=(o_l, nat_w_o[j], nat_b_o[j]))
            if update_ctx:
                o_c = _nat_attn(q_c, k_c, vt_c)
                tail_c = dict(proj=(o_c, nat_w_o[j], nat_b_o[j]))

        x = _ffn(x, mx, norm_g[i, 2], w_in, w_out, (i, 1), final_g=final_g if last else None, **tail_x)
        if update_ctx:
            xc = _ffn(xc, mc, norm_g[i, 2], w_in, w_out, (i, 1), **tail_c)
    return x
```

```python
import functools
import math

import jax
import jax.numpy as jnp
from jax import lax
from jax.experimental import pallas as pl
from jax.experimental.pallas import tpu as pltpu

D_MODEL = 1024
DEPTH = 4
GRID_W = 64
N_MIXERS = 3
D_FF = 2816
N_MOD = 9
POOL_WINDOWS = (2, 4, 8, 16)
POOL_GW = 256
POOL_HALO = 8
POOL_SUBTILE = 256

TOKEN_TILE = 512
WIDE_TILE = 1024
DIFF_TQ = 256
DIFF_TK = 1024
DIFF_TILES_PER_STEP = 4
NAT_BLOCKS_PER_STEP = 16
DIFF_HEADS = 8
DIFF_HD = 64
NAT_HEADS = 16
NAT_HD = 64
NAT_WIN_ROWS = 8
NAT_WIN_COLS = 16
NAT_QROWS = 4
NAT_BAND = 12
ROPE_THETA = 10000.0
NORM_EPS = 1e-6
NEG = -1e30
LOG2_E = math.log2(math.e)
BOUND_MARGIN = 1.01
FIXED_OFFSET_SPAN = 64.0

LANE = 128
FF_CHUNK = 256
VMEM_LIMIT = 56 * 1024 * 1024

F32 = jnp.float32
BF16 = jnp.bfloat16


def _cparams(n_axes):
    return pltpu.CompilerParams(dimension_semantics=("arbitrary",) * n_axes, vmem_limit_bytes=VMEM_LIMIT)


def _const_spec(shape):
    nd = len(shape)
    return pl.BlockSpec(shape, lambda *_: (0,) * nd, pipeline_mode=pl.Buffered(1))


def _mod_spec(mod):
    if mod.shape[0] == 1:
        return pl.BlockSpec((1, N_MOD, D_MODEL), lambda b, i: (0, 0, 0))
    return pl.BlockSpec((1, N_MOD, D_MODEL), lambda b, i: (b, 0, 0))


def _sigmoid(x):
    return 1.0 / (1.0 + jnp.exp(-x))


def _rms(x):
    return x * lax.rsqrt(jnp.mean(x * x, axis=-1, keepdims=True) + NORM_EPS)


def _norm_mod(x, g, mod_ref, row):
    shift = mod_ref[0, row:row + 1, :]
    scale = mod_ref[0, row + 1:row + 2, :]
    return (_rms(x) * g) * (1.0 + scale) + shift


def _ada_kernel(c_ref, w_ref, b_ref, o_ref):
    c = c_ref[...]
    s = (c * _sigmoid(c)).astype(BF16)
    o_ref[0] = jnp.dot(s, w_ref[0].astype(BF16), preferred_element_type=F32) + b_ref[0]


def _ada_mods(c, c_ctx, ada_w, ada_b):
    d = D_MODEL
    cc = jnp.concatenate([c, c_ctx[None], jnp.zeros((3, d), F32)], axis=0)
    tn = WIDE_TILE
    return pl.pallas_call(
        _ada_kernel,
        out_shape=jax.ShapeDtypeStruct((DEPTH, 8, N_MOD * d), F32),
        grid=(DEPTH, N_MOD * d // tn),
        in_specs=[pl.BlockSpec((8, d), lambda l, n: (0, 0)),
                  pl.BlockSpec((1, d, tn), lambda l, n: (l, 0, n)),
                  pl.BlockSpec((1, 1, tn), lambda l, n: (l, 0, n))],
        out_specs=pl.BlockSpec((1, 8, tn), lambda l, n: (l, 0, n)),
        compiler_params=_cparams(2), name="ada_mods",
    )(cc, ada_w, ada_b.reshape(DEPTH, 1, N_MOD * d))


def _swiglu_half_step(x, mod_ref, g_ref, win_ref, wout_ref, a_ref, row, r0=0, side_work=()):
    rows = slice(r0, r0 + x.shape[0])
    side_work = list(side_work)
    h = _norm_mod(x, g_ref[...], mod_ref, row).astype(BF16)
    for j in range(D_FF // FF_CHUNK):
        lo = j * FF_CHUNK
        g = jnp.dot(h, win_ref[0, 0, :, lo:lo + FF_CHUNK], preferred_element_type=F32)
        u = jnp.dot(h, win_ref[0, 0, :, D_FF + lo:D_FF + lo + FF_CHUNK], preferred_element_type=F32)
        a_ref[rows, lo:lo + FF_CHUNK] = ((g * _sigmoid(g)) * u).astype(BF16)
        if side_work and j % 2 == 1:
            side_work.pop(0)()
    for thunk in side_work:
        thunk()
    y = jnp.dot(a_ref[rows, :], wout_ref[0, 0], preferred_element_type=F32)
    return x + (0.5 * mod_ref[0, row + 2:row + 3, :]) * y


def _pool_fill(x_ref, xp_ref, xn_ref, mod_ref, g_ref, ext_ref):
    i = pl.program_id(1)
    last = pl.num_programs(1) - 1
    tm = x_ref.shape[1]
    g = g_ref[...]
    ext_ref[0:POOL_HALO, :] = jnp.where(i > 0, _norm_mod(xp_ref[0], g, mod_ref, 3), 0.0)
    ext_ref[POOL_HALO:POOL_HALO + tm, :] = _norm_mod(x_ref[0], g, mod_ref, 3)
    ext_ref[POOL_HALO + tm:, :] = jnp.where(i < last, _norm_mod(xn_ref[0], g, mod_ref, 3), 0.0)


def _pool_group(x_ref, mod_ref, pw_ref, ps_ref, ext_ref, seq, r0, n, gi):
    tm = x_ref.shape[1]
    win = POOL_WINDOWS[gi]
    t = pl.program_id(1) * tm + r0 + lax.broadcasted_iota(jnp.int32, (n, 1), 0)
    base = POOL_HALO + r0
    cols = slice(gi * POOL_GW, (gi + 1) * POOL_GW)
    lo_off = -(win // 2)
    acc = ext_ref[base + lo_off:base + lo_off + n, cols]
    for o in range(lo_off + 1, lo_off + win):
        acc = acc + ext_ref[base + o:base + o + n, cols]
    lo = jnp.clip(t + lo_off, 0, seq - 1)
    hi = jnp.clip(t + lo_off + win - 1, 0, seq - 1)
    cnt = (hi - lo + 1).astype(F32)
    diff = (acc / cnt - ext_ref[base:base + n, cols]).astype(BF16)
    y = jnp.dot(diff, pw_ref[gi], preferred_element_type=F32) * ps_ref[:, cols]
    return x_ref[0, r0:r0 + n, cols] + mod_ref[0, 5:6, cols] * y


def _ffn_kernel(*refs, mixer, seq, bias, final):
    refs = list(refs)
    x_ref = refs.pop(0)
    if mixer == "pool":
        xp_ref, xn_ref = refs.pop(0), refs.pop(0)
    elif mixer == "proj":
        attn_ref = refs.pop(0)
    mod_ref = refs.pop(0)
    if mixer == "pool":
        g1_ref, pw_ref, ps_ref = refs.pop(0), refs.pop(0), refs.pop(0)
    elif mixer == "proj":
        wo_ref = refs.pop(0)
        bo_ref = refs.pop(0) if bias else None
    g_ref, win_ref, wout_ref = refs.pop(0), refs.pop(0), refs.pop(0)
    fg_ref = refs.pop(0) if final else None
    o_ref, a_ref = refs.pop(0), refs.pop(0)
    tm = x_ref.shape[1]

    def finish(out):
        return _rms(out) * fg_ref[...] if final else out

    if mixer == "pool":
        ext_ref = refs.pop(0)
        n = min(tm, POOL_SUBTILE)
        n_groups = len(POOL_WINDOWS)
        _pool_fill(x_ref, xp_ref, xn_ref, mod_ref, g1_ref, ext_ref)
        group = functools.partial(_pool_group, x_ref, mod_ref, pw_ref, ps_ref, ext_ref, seq)
        cur = [group(0, n, gi) for gi in range(n_groups)]
        for r0 in range(0, tm, n):
            nxt, side = [], []
            if r0 + n < tm:
                side = [lambda gi=gi, r=r0 + n: nxt.append(group(r, n, gi)) for gi in range(n_groups)]
            out = _swiglu_half_step(jnp.concatenate(cur, axis=1), mod_ref, g_ref, win_ref, wout_ref, a_ref, 6, r0, side)
            o_ref[0, r0:r0 + n, :] = finish(out)
            cur = nxt
        return
    if mixer == "proj":
        y = jnp.dot(attn_ref[0], wo_ref[...], preferred_element_type=F32)
        if bias:
            y = y + bo_ref[...]
        x = x_ref[0] + mod_ref[0, 5:6, :] * y
        row = 6
    else:
        x = x_ref[0]
        row = 0
    o_ref[0] = finish(_swiglu_half_step(x, mod_ref, g_ref, win_ref, wout_ref, a_ref, row))


def _ffn(x, mod, g, w_in, w_out, widx, pool=None, proj=None, final_g=None):
    bx, sx, d = x.shape
    tm = min(TOKEN_TILE, sx)
    tok = pl.BlockSpec((1, tm, d), lambda b, i: (b, i, 0))
    in_specs, args, scratch = [tok], [x], [pltpu.VMEM((tm, D_FF), BF16)]
    mixer = None
    if pool is not None:
        mixer = "pool"
        nb, n8 = tm // POOL_HALO, sx // POOL_HALO
        in_specs += [pl.BlockSpec((1, POOL_HALO, d), lambda b, i: (b, jnp.maximum(i * nb - 1, 0), 0)),
                     pl.BlockSpec((1, POOL_HALO, d), lambda b, i: (b, jnp.minimum((i + 1) * nb, n8 - 1), 0))]
        args += [x, x]
        scratch.append(pltpu.VMEM((tm + 2 * POOL_HALO, d), F32))
    elif proj is not None:
        mixer = "proj"
        in_specs.append(tok)
        args.append(proj[0])
    in_specs.append(_mod_spec(mod))
    args.append(mod)
    if pool is not None:
        in_specs += [_const_spec((1, d)), _const_spec((len(POOL_WINDOWS), POOL_GW, POOL_GW)), _const_spec((1, d))]
        args += [pool[0].reshape(1, d), pool[1], pool[2].reshape(1, d)]
    elif proj is not None:
        in_specs.append(_const_spec((d, d)))
        args.append(proj[1].astype(BF16))
        if proj[2] is not None:
            in_specs.append(_const_spec((1, d)))
            args.append(proj[2].reshape(1, d))
    in_specs += [_const_spec((1, d)),
                 pl.BlockSpec((1, 1, d, 2 * D_FF), lambda b, i: widx + (0, 0), pipeline_mode=pl.Buffered(1)),
                 pl.BlockSpec((1, 1, D_FF, d), lambda b, i: widx + (0, 0), pipeline_mode=pl.Buffered(1))]
    args += [g.reshape(1, d), w_in, w_out]
    if final_g is not None:
        in_specs.append(_const_spec((1, d)))
        args.append(final_g.reshape(1, d))
    return pl.pallas_call(
        functools.partial(_ffn_kernel, mixer=mixer, seq=sx, bias=proj is not None and proj[2] is not None,
                          final=final_g is not None),
        out_shape=jax.ShapeDtypeStruct(x.shape, F32),
        grid=(bx, sx // tm),
        in_specs=in_specs,
        out_specs=tok,
        scratch_shapes=scratch,
        compiler_params=_cparams(2), name="ffn" if mixer is None else mixer + "_ffn",
    )(*args)


def _rope_table_kernel(inv_ref, cos_ref, sin_ref):
    tm = cos_ref.shape[0]
    t = pl.program_id(0) * tm + lax.broadcasted_iota(jnp.int32, (tm, LANE), 0)
    lane = lax.broadcasted_iota(jnp.int32, (tm, LANE), 1)
    pos = jnp.where((lane & 63) < 32, lax.shift_right_logical(t, 6), t & (GRID_W - 1)).astype(F32)
    ang = pos * inv_ref[...]
    cos_ref[...] = jnp.cos(ang)
    sin_ref[...] = jnp.where((lane & 31) < 16, -jnp.sin(ang), jnp.sin(ang))


def _rope_tables(seq):
    per_axis = DIFF_HD // 2
    inv = ROPE_THETA ** (-jnp.arange(0, per_axis, 2, dtype=F32) / per_axis)
    inv_lane = jnp.tile(inv, LANE // inv.shape[0]).reshape(1, LANE)
    tm = min(WIDE_TILE, seq)
    return pl.pallas_call(
        _rope_table_kernel,
        out_shape=(jax.ShapeDtypeStruct((seq, LANE), F32),) * 2,
        grid=(seq // tm,),
        in_specs=[pl.BlockSpec((1, LANE), lambda i: (0, 0))],
        out_specs=(pl.BlockSpec((tm, LANE), lambda i: (i, 0)),) * 2,
        compiler_params=_cparams(1), name="rope_tables",
    )(inv_lane)


def _qkv_kernel(x_ref, mod_ref, g_ref, wqk_ref, wvt_ref, *rest, rope, bias, scale):
    rest = list(rest)
    if bias:
        bqk_ref, bvt_ref = rest[:2]
        rest = rest[2:]
    if rope:
        cos_ref, sin_ref = rest[:2]
        rest = rest[2:]
    q_ref, k_ref, vt_ref = rest
    d = x_ref.shape[2]
    h = _norm_mod(x_ref[0], g_ref[...], mod_ref, 3).astype(BF16)
    vt = lax.dot_general(wvt_ref[...], h, (((1,), (1,)), ((), ())), preferred_element_type=F32)
    if bias:
        vt = vt + bvt_ref[...]
    vt_ref[0] = vt.astype(BF16)
    if rope:
        cos = cos_ref[...]
        sin = sin_ref[...]
        lane = lax.broadcasted_iota(jnp.int32, cos.shape, 1)
        first_half = (lane & 31) < 16
    for cb in range(2 * d // FF_CHUNK):
        c0 = cb * FF_CHUNK
        t2 = jnp.dot(h, wqk_ref[:, c0:c0 + FF_CHUNK], preferred_element_type=F32)
        if bias:
            t2 = t2 + bqk_ref[:, c0:c0 + FF_CHUNK]
        for half in range(FF_CHUNK // LANE):
            t = t2[:, half * LANE:(half + 1) * LANE]
            if rope:
                partner = jnp.where(first_half, pltpu.roll(t, LANE - 16, 1), pltpu.roll(t, 16, 1))
                t = t * cos + partner * sin
            col = c0 + half * LANE
            if col < d:
                q_ref[0, :, col:col + LANE] = (t * scale).astype(BF16)
            else:
                k_ref[0, :, col - d:col - d + LANE] = t.astype(BF16)


def _qkv(x, mod, g, w_qkv, scale, b_qkv=None, rope_tabs=None):
    bx, sx, d = x.shape
    tm = min(TOKEN_TILE, sx)
    wqk = w_qkv[:, :2 * d].astype(BF16)
    wvt = w_qkv[:, 2 * d:].T.astype(BF16)
    in_specs = [pl.BlockSpec((1, tm, d), lambda b, i: (b, i, 0)), _mod_spec(mod), _const_spec((1, d)),
                _const_spec((d, 2 * d)), _const_spec((d, d))]
    args = [x, mod, g.reshape(1, d), wqk, wvt]
    if b_qkv is not None:
        in_specs += [_const_spec((1, 2 * d)), _const_spec((d, 1))]
        args += [b_qkv[:2 * d].reshape(1, 2 * d), b_qkv[2 * d:].reshape(d, 1)]
    if rope_tabs is not None:
        in_specs += [pl.BlockSpec((tm, LANE), lambda b, i: (i, 0))] * 2
        args += list(rope_tabs)
    tok = pl.BlockSpec((1, tm, d), lambda b, i: (b, i, 0))
    return pl.pallas_call(
        functools.partial(_qkv_kernel, rope=rope_tabs is not None, bias=b_qkv is not None, scale=scale),
        out_shape=(jax.ShapeDtypeStruct((bx, sx, d), BF16), jax.ShapeDtypeStruct((bx, sx, d), BF16),
                   jax.ShapeDtypeStruct((bx, d, sx), BF16)),
        grid=(bx, sx // tm),
        in_specs=in_specs,
        out_specs=(tok, tok, pl.BlockSpec((1, d, tm), lambda b, i: (b, 0, i))),
        compiler_params=_cparams(2), name="qkv_proj",
    )(*args)


def _subhead_sq_norms(x, ones):
    xf = x.astype(F32)
    return jnp.dot((xf * xf).astype(BF16), ones, preferred_element_type=F32)


def _diff_attn_kernel(lam_ref, g_ref, q_ref, kc_ref, vct_ref, *rest, lam_init, tk, tq):
    if len(rest) > 2:
        k_ref, vt_ref, o_ref, acc_ref, sa_ref, sb_ref, sc_ref, ml_ref, kn_ref, *more_acc_refs = rest
    else:
        k_ref = vt_ref = None
        o_ref, acc_ref = rest
        sa_ref = sb_ref = sc_ref = ml_ref = kn_ref = None
    n_sub = q_ref.shape[1] // tq
    hw = 2 * DIFF_HD
    n_chunks = None

    if k_ref is not None:
        n_chunks = k_ref.shape[1] // tk

        @pl.when(pl.program_id(2) == 0)
        def _():
            r = lax.broadcasted_iota(jnp.int32, (hw, hw), 0)
            c_ = lax.broadcasted_iota(jnp.int32, (hw, hw), 1)
            same_subhead = jnp.where((r < DIFF_HD) == (c_ < DIFF_HD), 1.0, 0.0).astype(BF16)
            kn = jnp.max(_subhead_sq_norms(kc_ref[0], same_subhead), axis=0, keepdims=True)
            for c in range(n_chunks):
                kn = jnp.maximum(kn, jnp.max(_subhead_sq_norms(k_ref[0, c * tk:(c + 1) * tk, :], same_subhead),
                                             axis=0, keepdims=True))
            kn_ref[...] = kn

    def split_subheads(rows):
        q = q_ref[0, rows, :]
        lane = lax.broadcasted_iota(jnp.int32, q.shape, 1)
        zero = jnp.zeros_like(q)
        return q, jnp.concatenate([jnp.where(lane < DIFF_HD, q, zero), jnp.where(lane >= DIFF_HD, q, zero)], axis=0)

    def scores(k_t, qq):
        return lax.dot_general(k_t, qq, (((1,), (1,)), ((), ())), preferred_element_type=F32)

    def context_block(s, m, acc_ref=acc_ref):
        p = jnp.exp2(s - m)
        acc_ref[...] = jnp.dot(vct_ref[0], p.astype(BF16), preferred_element_type=F32)
        return jnp.sum(p, axis=0, keepdims=True)

    def finish(rows, l, acc_ref=acc_ref):
        lf = lam_ref[...]
        lam = (jnp.exp(jnp.sum(lf[0:1] * lf[1:2], axis=-1, keepdims=True))
               - jnp.exp(jnp.sum(lf[2:3] * lf[3:4], axis=-1, keepdims=True)) + lam_init)
        acc = acc_ref[...]
        inv = 1.0 / l
        o = acc[:, :tq] * inv[:, :tq] - lam * (acc[:, tq:] * inv[:, tq:])
        o = o * lax.rsqrt(jnp.mean(o * o, axis=0, keepdims=True) + NORM_EPS)
        o = (o * g_ref[...]) * (1.0 - lam_init)
        o_ref[0, rows, :] = o.T.astype(BF16)

    if k_ref is None:
        assert n_sub == 1
        rows = slice(0, tq)
        _, qq = split_subheads(rows)
        s = scores(kc_ref[0], qq)
        finish(rows, context_block(s, jnp.max(s, axis=0, keepdims=True)))
        return

    r = lax.broadcasted_iota(jnp.int32, (16, hw), 0)
    c_ = lax.broadcasted_iota(jnp.int32, (16, hw), 1)
    select = jnp.where(r == jnp.where(c_ < DIFF_HD, 0, 1), 1.0, 0.0).astype(BF16)
    kn = kn_ref[...]
    gap = None
    for t in range(n_sub):
        q, qq = split_subheads(slice(t * tq, (t + 1) * tq))
        s = scores(kc_ref[0], qq)
        m = jnp.max(s, axis=0, keepdims=True)
        qf = q.astype(F32)
        qn = lax.dot_general(select, (qf * qf).astype(BF16), (((1,), (1,)), ((), ())),
                             preferred_element_type=F32)
        bound = jnp.concatenate([jnp.sqrt(qn[0:1, :] * kn[:, 0:1]),
                                 jnp.sqrt(qn[1:2, :] * kn[:, DIFF_HD:DIFF_HD + 1])], axis=1) * BOUND_MARGIN
        sc_ref[t] = s
        ml_ref[t, 0:1, :] = m
        ml_ref[t, 1:2, :] = bound
        gap = bound - m if gap is None else jnp.maximum(gap, bound - m)
    fixed = jnp.max(gap) <= FIXED_OFFSET_SPAN

    def run_tiles(online):
        def produce(c, s_ref, qq):
            s = scores(k_ref[0, c * tk:(c + 1) * tk, :], qq)
            s_ref[...] = s
            return jnp.max(s, axis=0, keepdims=True) if online else None

        def consume(s_ref, c, s_max, m, l):
            if online:
                m_new = jnp.maximum(m, s_max)
                alpha = jnp.exp2(m - m_new)
                p = jnp.exp2(s_ref[...] - m_new)
                l = alpha * l + jnp.sum(p, axis=0, keepdims=True)
                pv_prev = alpha * acc_ref[...]
            else:
                p = jnp.exp2(s_ref[...] - m)
                l = l + jnp.sum(p, axis=0, keepdims=True)
                pv_prev = acc_ref[...]
                m_new = m
            vt_t = vt_ref[0, :, c * tk:(c + 1) * tk]
            acc_ref[...] = pv_prev + jnp.dot(vt_t, p.astype(BF16), preferred_element_type=F32)
            return m_new, l

        def fixed_offset_tile(t, acc):
            rows = slice(t * tq, (t + 1) * tq)
            _, qq = split_subheads(rows)
            m = ml_ref[t, 1:2, :]
            l = context_block(sc_ref[t], m, acc)
            for c in range(n_chunks):
                p = jnp.exp2(scores(k_ref[0, c * tk:(c + 1) * tk, :], qq) - m)
                l = l + jnp.sum(p, axis=0, keepdims=True)
                acc[...] += jnp.dot(vt_ref[0, :, c * tk:(c + 1) * tk], p.astype(BF16), preferred_element_type=F32)
            finish(rows, l, acc)

        if not online:
            for t, acc in enumerate([acc_ref] + more_acc_refs):
                fixed_offset_tile(t, acc)
            return

        def tile(t, carry):
            rows = pl.ds(pl.multiple_of(t * tq, tq), tq)
            _, qq = split_subheads(rows)
            m = ml_ref[t, 0:1, :]
            bufs = (sa_ref, sb_ref)
            s_max = produce(0, bufs[0], qq)
            l = context_block(sc_ref[t], m)
            for c in range(n_chunks):
                if c + 1 < n_chunks:
                    next_max = produce(c + 1, bufs[(c + 1) % 2], qq)
                m, l = consume(bufs[c % 2], c, s_max, m, l)
                s_max = next_max
            finish(rows, l)
            return carry

        lax.fori_loop(0, n_sub, tile, 0)

    pl.when(fixed)(functools.partial(run_tiles, False))
    pl.when(jnp.logical_not(fixed))(functools.partial(run_tiles, True))


def _diff_attn(lam, subln_g, q, kc, vct, k, vt, lam_init):
    bq, sq, d = q.shape
    n_ctx = kc.shape[1]
    tq = DIFF_TQ
    tq_step = tq * math.gcd(sq // tq, DIFF_TILES_PER_STEP)
    hw = 2 * DIFF_HD
    in_specs = [_const_spec((4, DIFF_HD)), _const_spec((hw, 1)),
                pl.BlockSpec((1, tq_step, hw), lambda b, h, i: (b, i, h)),
                pl.BlockSpec((1, n_ctx, hw), lambda b, h, i: (b, 0, h)),
                pl.BlockSpec((1, hw, n_ctx), lambda b, h, i: (b, h, 0))]
    args = [lam, subln_g.reshape(hw, 1), q, kc, vct]
    scratch = [pltpu.VMEM((hw, 2 * tq), F32)]
    tk = DIFF_TK
    if k is not None:
        s = k.shape[1]
        assert s % tk == 0
        in_specs += [pl.BlockSpec((1, s, hw), lambda b, h, i: (b, 0, h)),
                     pl.BlockSpec((1, hw, s), lambda b, h, i: (b, h, 0))]
        args += [k, vt]
        n_sub = tq_step // tq
        scratch += [pltpu.VMEM((tk, 2 * tq), F32)] * 2 + [pltpu.VMEM((n_sub, n_ctx, 2 * tq), F32),
                                                          pltpu.VMEM((n_sub, 2, 2 * tq), F32), pltpu.VMEM((1, hw), F32)]
        scratch += [pltpu.VMEM((hw, 2 * tq), F32)] * (n_sub - 1)
    return pl.pallas_call(
        functools.partial(_diff_attn_kernel, lam_init=lam_init, tk=tk, tq=tq),
        out_shape=jax.ShapeDtypeStruct((bq, sq, d), BF16),
        grid=(bq, DIFF_HEADS, sq // tq_step),
        in_specs=in_specs,
        out_specs=pl.BlockSpec((1, tq_step, hw), lambda b, h, i: (b, i, h)),
        scratch_shapes=scratch,
        compiler_params=_cparams(3), name="diff_attn",
    )(*args)


def _nat_bias_rows():
    table = []
    for v, shift in enumerate((0, 4, 8)):
        rows = []
        for i in range(NAT_BAND):
            per_d = []
            for dq in range(NAT_QROWS):
                first = (0, dq, 4)[v]
                ok = first <= i < first + NAT_WIN_ROWS
                per_d.append(i - shift - dq + NAT_WIN_ROWS - 1 if ok else None)
            rows.append(per_d)
        table.append(rows)
    return table


def _nat_toeplitz_kernel(r_ref, o_ref):
    m, n = o_ref.shape

    def window(shape):
        col = lax.broadcasted_iota(jnp.int32, shape, 1)
        kc = lax.shift_right_logical(col, 7)
        qc = col & (GRID_W - 1)
        cs = jnp.clip(qc - NAT_WIN_COLS // 2, 0, GRID_W - NAT_WIN_COLS)
        return jnp.where(kc >= cs, jnp.where(kc < cs + NAT_WIN_COLS, kc - qc + NAT_WIN_COLS - 1, -1), -1)

    j = lax.broadcasted_iota(jnp.int32, (LANE, n), 0)
    onehot = jnp.where(j == window((LANE, n)), 1.0, 0.0).astype(BF16)
    r = r_ref[...]
    r_hi = r.astype(BF16)
    r1 = r - r_hi.astype(F32)
    r_mid = r1.astype(BF16)
    r_lo = (r1 - r_mid.astype(F32)).astype(BF16)
    t = (jnp.dot(r_hi, onehot, preferred_element_type=F32) + jnp.dot(r_mid, onehot, preferred_element_type=F32)
         + jnp.dot(r_lo, onehot, preferred_element_type=F32))
    o_ref[...] = jnp.where(window((m, n)) >= 0, t * LOG2_E, NEG)


def _nat_bias_kernel(t_ref, o_ref):
    lane = lax.broadcasted_iota(jnp.int32, (GRID_W, LANE), 1)
    neg = jnp.full((GRID_W, LANE), NEG, F32)
    rows = _nat_bias_rows()
    for v in range(3):
        for i in range(NAT_BAND):
            for pair in range(NAT_QROWS // 2):
                a_l, a_r = rows[v][i][2 * pair], rows[v][i][2 * pair + 1]
                left = neg if a_l is None else t_ref[0, a_l]
                right = neg if a_r is None else t_ref[0, a_r]
                o_ref[0, v, i * GRID_W:(i + 1) * GRID_W, pair * LANE:(pair + 1) * LANE] = (
                    jnp.where(lane < GRID_W, left, right))


def _nat_bias(rpb):
    nh, nr, nc = rpb.shape
    r2 = jnp.pad(rpb, ((0, 0), (0, 16 - nr), (0, LANE - nc))).reshape(nh * 16, LANE)
    n = GRID_W * LANE
    t = pl.pallas_call(
        _nat_toeplitz_kernel,
        out_shape=jax.ShapeDtypeStruct((nh * 16, n), F32),
        grid=(1,),
        in_specs=[pl.BlockSpec((nh * 16, LANE), lambda i: (0, 0))],
        out_specs=pl.BlockSpec((nh * 16, n), lambda i: (0, 0)),
        compiler_params=_cparams(1), name="nat_toeplitz",
    )(r2)
    t = t.reshape(nh, 16, GRID_W, LANE)
    nk, nq = NAT_BAND * GRID_W, NAT_QROWS * GRID_W
    return pl.pallas_call(
        _nat_bias_kernel,
        out_shape=jax.ShapeDtypeStruct((nh, 3, nk, nq), F32),
        grid=(nh,),
        in_specs=[pl.BlockSpec((1, 16, GRID_W, LANE), lambda h: (h, 0, 0, 0))],
        out_specs=pl.BlockSpec((1, 3, nk, nq), lambda h: (h, 0, 0, 0)),
        compiler_params=_cparams(1), name="nat_bias",
    )(t)


def _nat_attn_kernel(q_ref, kc_ref, vct_ref, *rest, n_rows, n_blk, group):
    if len(rest) == 6:
        bias_ref, k_ref, vt_ref, o_ref, sa_ref, sb_ref = rest
    else:
        bias_ref = k_ref = vt_ref = None
        o_ref, sa_ref, sb_ref = rest
    tq = NAT_QROWS * GRID_W
    nk = NAT_BAND * GRID_W
    n_ctx = kc_ref.shape[1]
    n_heads = LANE // NAT_HD
    dn = (((1,), (1,)), ((), ()))
    lane = lax.broadcasted_iota(jnp.int32, (tq, LANE), 1)
    zero = jnp.zeros((tq, LANE), BF16)

    def band(g):
        blk = pl.program_id(2) * group + g
        start = jnp.clip(blk * NAT_QROWS - NAT_WIN_ROWS // 2, 0, n_rows - NAT_BAND)
        return (pl.multiple_of(start * GRID_W, NAT_QROWS * GRID_W),
                jnp.where(blk == 0, 0, jnp.where(blk == n_blk - 1, 2, 1)))

    def produce(g, hh, s_ref):
        q = q_ref[0, g * tq:(g + 1) * tq, :]
        qh = jnp.where((lane >= hh * NAT_HD) & (lane < (hh + 1) * NAT_HD), q, zero)
        s_c = lax.dot_general(kc_ref[0], qh, dn, preferred_element_type=F32)
        s_ref[0:n_ctx, :] = s_c
        m = jnp.max(s_c, axis=0, keepdims=True)
        if k_ref is not None:
            off, variant = band(g)
            s_b = lax.dot_general(k_ref[0, pl.ds(off, nk), :], qh, dn, preferred_element_type=F32)
            s_b = s_b + bias_ref[hh, variant]
            s_ref[n_ctx:, :] = s_b
            m = jnp.maximum(m, jnp.max(s_b, axis=0, keepdims=True))
        return m

    def consume(g, hh, s_ref, m):
        rows = slice(hh * NAT_HD, (hh + 1) * NAT_HD)
        p = jnp.exp2(s_ref[...] - m)
        l = jnp.sum(p, axis=0, keepdims=True)
        pb = p.astype(BF16)
        o = jnp.dot(vct_ref[0, rows, :], pb[0:n_ctx], preferred_element_type=F32)
        if k_ref is not None:
            off, _ = band(g)
            o = o + jnp.dot(vt_ref[0, rows, pl.ds(off, nk)], pb[n_ctx:], preferred_element_type=F32)
        return o / l

    chains = [(g, hh) for g in range(group) for hh in range(n_heads)]
    bufs = (sa_ref, sb_ref)
    m = produce(*chains[0], bufs[0])
    outs = []
    for i, (g, hh) in enumerate(chains):
        if i + 1 < len(chains):
            m_next = produce(*chains[i + 1], bufs[(i + 1) % 2])
        outs.append(consume(g, hh, bufs[i % 2], m))
        m = m_next
        if hh == n_heads - 1:
            o_ref[0, g * tq:(g + 1) * tq, :] = jnp.concatenate(outs, axis=0).T.astype(BF16)
            outs = []


def _nat_attn(q, kc, vct, bias=None, k=None, vt=None):
    bq, sq, d = q.shape
    n_ctx = kc.shape[1]
    tq = NAT_QROWS * GRID_W
    n_blk = sq // tq
    group = math.gcd(n_blk, NAT_BLOCKS_PER_STEP)
    in_specs = [pl.BlockSpec((1, group * tq, LANE), lambda hp, b, i: (b, i, hp)),
                pl.BlockSpec((1, n_ctx, LANE), lambda hp, b, i: (b, 0, hp)),
                pl.BlockSpec((1, LANE, n_ctx), lambda hp, b, i: (b, hp, 0))]
    args = [q, kc, vct]
    n_rows = None
    n_keys = n_ctx
    if k is not None:
        s = k.shape[1]
        n_rows = s // GRID_W
        nk = NAT_BAND * GRID_W
        n_keys += nk
        in_specs += [pl.BlockSpec((LANE // NAT_HD, 3, nk, tq), lambda hp, b, i: (hp, 0, 0, 0)),
                     pl.BlockSpec((1, s, LANE), lambda hp, b, i: (b, 0, hp)),
                     pl.BlockSpec((1, LANE, s), lambda hp, b, i: (b, hp, 0))]
        args += [bias, k, vt]
    return pl.pallas_call(
        functools.partial(_nat_attn_kernel, n_rows=n_rows, n_blk=n_blk, group=group),
        out_shape=jax.ShapeDtypeStruct((bq, sq, d), BF16),
        grid=(d // LANE, bq, n_blk // group),
        in_specs=in_specs,
        out_specs=pl.BlockSpec((1, group * tq, LANE), lambda hp, b, i: (b, i, hp)),
        scratch_shapes=[pltpu.VMEM((n_keys, tq), F32)] * 2,
        compiler_params=_cparams(3), name="nat_attn",
    )(*args)


def kernel(x, c, ctx, c_ctx, ada_w, ada_b, norm_g, ffn_w_in, ffn_w_out, pool_w, pool_scale,
           diff_w_qkv, diff_lam, diff_subln_g, diff_w_o, nat_w_qkv, nat_b_qkv, nat_rpb,
           nat_w_o, nat_b_o, final_g):
    b, seq, d = x.shape
    mods = _ada_mods(c, c_ctx, ada_w, ada_b)
    w_in = ffn_w_in.astype(BF16)
    w_out = ffn_w_out.astype(BF16)
    xc = ctx
    for i in range(DEPTH):
        kind = i % N_MIXERS
        j = i // N_MIXERS
        last = i == DEPTH - 1
        update_ctx = not last
        ctx_needed = update_ctx or kind != 0
        mx = mods[i, :b].reshape(b, N_MOD, d)
        mc = mods[i, b:b + 1].reshape(1, N_MOD, d)

        x = _ffn(x, mx, norm_g[i, 0], w_in, w_out, (i, 0))
        if ctx_needed:
            xc = _ffn(xc, mc, norm_g[i, 0], w_in, w_out, (i, 0))

        if kind == 0:
            tail_x = tail_c = dict(pool=(norm_g[i, 1], pool_w[j].astype(BF16), pool_scale[j]))
        elif kind == 1:
            lam_init = 0.8 - 0.6 * math.exp(-0.3 * i)
            scale = DIFF_HD ** -0.5 * LOG2_E
            q_l, k_l, vt_l = _qkv(x, mx, norm_g[i, 1], diff_w_qkv[j], scale, rope_tabs=_rope_tables(seq))
            q_c, k_c, vt_c = _qkv(xc, mc, norm_g[i, 1], diff_w_qkv[j], scale)
            o_l = _diff_attn(diff_lam[j], diff_subln_g[j], q_l, k_c, vt_c, k_l, vt_l, lam_init)
            tail_x = dict(proj=(o_l, diff_w_o[j], None))
            if update_ctx:
                o_c = _diff_attn(diff_lam[j], diff_subln_g[j], q_c, k_c, vt_c, None, None, lam_init)
                tail_c = dict(proj=(o_c, diff_w_o[j], None))
        else:
            scale = NAT_HD ** -0.5 * LOG2_E
            q_l, k_l, vt_l = _qkv(x, mx, norm_g[i, 1], nat_w_qkv[j], scale, b_qkv=nat_b_qkv[j])
            q_c, k_c, vt_c = _qkv(xc, mc, norm_g[i, 1], nat_w_qkv[j], scale, b_qkv=nat_b_qkv[j])
            o_l = _nat_attn(q_l, k_c, vt_c, _nat_bias(nat_rpb[j]), k_l, vt_l)
            tail_x = dict(proj=(o_l, nat_w_o[j], nat_b_o[j]))
            if update_ctx:
                o_c = _nat_attn(q_c, k_c, vt_c)
                tail_c = dict(proj=(o_c, nat_w_o[j], nat_b_o[j]))

        x = _ffn(x, mx, norm_g[i, 2], w_in, w_out, (i, 1), final_g=final_g if last else None, **tail_x)
        if update_ctx:
            xc = _ffn(xc, mc, norm_g[i, 2], w_in, w_out, (i, 1), **tail_c)
    return x
```
